```python
import jax, jax.numpy as jnp
from jax import lax
import numpy as np

D_MODEL = 2048
BATCH = 2
SEQ = 16384
DEPTH = 2

D_FF = 5632
N_MIXERS = 2
N_GLA = (DEPTH + 1) // 2
N_POOL = DEPTH // 2
GLA_HEADS = 4
GLA_HEAD_K = D_MODEL // (2 * GLA_HEADS)
GLA_HEAD_V = D_MODEL // GLA_HEADS
GLA_K = GLA_HEADS * GLA_HEAD_K
GLA_V = GLA_HEADS * GLA_HEAD_V
GLA_RANK = 16
GLA_TAU = 16.0
GLA_CHUNK = 64
GLA_SPLITS = (GLA_K, 2 * GLA_K, 2 * GLA_K + GLA_V, 2 * GLA_K + 2 * GLA_V,
              2 * GLA_K + 2 * GLA_V + GLA_RANK)
GLA_IN = 2 * GLA_K + 2 * GLA_V + 2 * GLA_RANK
POOL_WINDOWS = (2, 4, 8, 16)
POOL_GROUPS = len(POOL_WINDOWS)
POOL_G = D_MODEL // POOL_GROUPS
NORM_EPS = 1e-6

kernel_name = "hybrid_gla_pool_macaron_encoder"


def _rmsnorm(x, g):
    x32 = x.astype(jnp.float32)
    y = x32 * lax.rsqrt(jnp.mean(x32 * x32, axis=-1, keepdims=True) + NORM_EPS)
    return (y * g.astype(jnp.float32)).astype(x.dtype)


def _swiglu(x, w_gate, w_up, w_down):
    return (jax.nn.silu(x @ w_gate) * (x @ w_up)) @ w_down


def _gla_one_direction(q, k, v, log_a):
    B, H, L, dk = q.shape
    dv = v.shape[-1]
    n = L // GLA_CHUNK
    C = GLA_CHUNK
    q = q.reshape(B, H, n, C, dk)
    k = k.reshape(B, H, n, C, dk)
    v = v.reshape(B, H, n, C, dv)
    b = jnp.cumsum(log_a.reshape(B, H, n, C, dk), axis=3)
    b_last = b[:, :, :, -1:, :]
    q_in = q * jnp.exp(b)
    k_in = k * jnp.exp(-b)
    k_end = k * jnp.exp(b_last - b)
    mask = jnp.tril(jnp.ones((C, C), dtype=bool))
    scores = jnp.where(mask, jnp.einsum('bhncd,bhnsd->bhncs', q_in, k_in), 0.0)
    o_intra = jnp.einsum('bhncs,bhnse->bhnce', scores, v)

    def step(S, inp):
        qc, kc, vc, dc = inp
        o = jnp.einsum('bhcd,bhde->bhce', qc, S)
        S = S * dc[..., None] + jnp.einsum('bhcd,bhce->bhde', kc, vc)
        return S, o

    xs = (jnp.moveaxis(q_in, 2, 0), jnp.moveaxis(k_end, 2, 0), jnp.moveaxis(v, 2, 0),
          jnp.moveaxis(jnp.exp(b_last[:, :, :, 0, :]), 2, 0))
    S0 = jnp.zeros((B, H, dk, dv), jnp.float32)
    _, o_inter = lax.scan(step, S0, xs)
    o = o_intra + jnp.moveaxis(o_inter, 0, 2)
    return o.reshape(B, H, L, dv)


def _gla_mixer(h, w_in, w_gate_up, b_gate, head_gain, w_out):
    B, L, _ = h.shape
    proj = h @ w_in
    q, k, v, r, z_f, z_b = jnp.split(proj, GLA_SPLITS, axis=-1)

    def heads(t, d):
        return t.reshape(B, L, GLA_HEADS, d).transpose(0, 2, 1, 3).astype(jnp.float32)

    log_a_f = jax.nn.log_sigmoid(z_f.astype(jnp.float32) @ w_gate_up[0].astype(jnp.float32)
                                 + b_gate[0].astype(jnp.float32)) / GLA_TAU
    log_a_b = jax.nn.log_sigmoid(z_b.astype(jnp.float32) @ w_gate_up[1].astype(jnp.float32)
                                 + b_gate[1].astype(jnp.float32)) / GLA_TAU
    qh = heads(q, GLA_HEAD_K) * (GLA_HEAD_K ** -0.5)
    kh = heads(k, GLA_HEAD_K)
    vh = heads(v, GLA_HEAD_V)
    flip = lambda t: jnp.flip(t, axis=2)
    o_f = _gla_one_direction(qh, kh, vh, heads(log_a_f, GLA_HEAD_K))
    o_b = flip(_gla_one_direction(flip(qh), flip(kh), flip(vh), flip(heads(log_a_b, GLA_HEAD_K))))
    o = o_f + o_b
    o = o * lax.rsqrt(jnp.mean(o * o, axis=-1, keepdims=True) + NORM_EPS) * head_gain.astype(jnp.float32)
    o = o.transpose(0, 2, 1, 3).reshape(B, L, GLA_V).astype(h.dtype)
    return (o * jax.nn.silu(r)) @ w_out


def _pool_mixer(h, w_pool, b_pool, scale):
    B, L, D = h.shape
    h32 = h.astype(jnp.float32)
    cs = jnp.concatenate([jnp.zeros((B, 1, D), jnp.float32), jnp.cumsum(h32, axis=1)], axis=1)
    t = jnp.arange(L)
    means = []
    for g, w in enumerate(POOL_WINDOWS):
        lo = jnp.clip(t - w // 2, 0, L)
        hi = jnp.clip(t + w // 2, 0, L)
        cg = cs[:, :, g * POOL_G:(g + 1) * POOL_G]
        s = jnp.take(cg, hi, axis=1) - jnp.take(cg, lo, axis=1)
        cnt = (hi - lo).astype(jnp.float32)[None, :, None]
        means.append(s / cnt)
    pooled = jnp.stack(means, axis=2) - h32.reshape(B, L, POOL_GROUPS, POOL_G)
    y = jnp.einsum('blgc,gcd->blgd', pooled.astype(h.dtype), w_pool).reshape(B, L, D)
    return (y + b_pool) * scale


def setup_inputs(seed: int = 0) -> dict:
    key = jax.random.key(seed)
    ks = jax.random.split(key, 16)
    nrm = jax.random.normal
    f32 = jnp.float32
    return {
        "x": nrm(ks[0], (BATCH, SEQ, D_MODEL), f32),
        "ffn_w_gate": nrm(ks[1], (DEPTH, 2, D_MODEL, D_FF), f32) * D_MODEL ** -0.5,
        "ffn_w_up": nrm(ks[2], (DEPTH, 2, D_MODEL, D_FF), f32) * D_MODEL ** -0.5,
        "ffn_w_down": nrm(ks[3], (DEPTH, 2, D_FF, D_MODEL), f32) * D_FF ** -0.5,
        "norm_gain": 1.0 + 0.05 * nrm(ks[4], (DEPTH, 6, D_MODEL), f32),
        "gla_w_in": nrm(ks[5], (N_GLA, D_MODEL, GLA_IN), f32) * D_MODEL ** -0.5,
        "gla_w_gate_up": nrm(ks[6], (N_GLA, 2, GLA_RANK, GLA_K), f32) * GLA_RANK ** -0.5,
        "gla_b_gate": 0.1 * nrm(ks[7], (N_GLA, 2, GLA_K), f32),
        "gla_head_gain": 1.0 + 0.05 * nrm(ks[8], (N_GLA, GLA_HEAD_V), f32),
        "gla_w_out": nrm(ks[9], (N_GLA, GLA_V, D_MODEL), f32) * GLA_V ** -0.5,
        "pool_w": nrm(ks[10], (N_POOL, POOL_GROUPS, POOL_G, POOL_G), f32) * POOL_G ** -0.5,
        "pool_b": 0.01 * nrm(ks[11], (N_POOL, D_MODEL), f32),
        "pool_scale": 1.0 + 0.1 * nrm(ks[12], (N_POOL, D_MODEL), f32),
    }


def reference(x, ffn_w_gate, ffn_w_up, ffn_w_down, norm_gain, gla_w_in, gla_w_gate_up,
              gla_b_gate, gla_head_gain, gla_w_out, pool_w, pool_b, pool_scale):
    for i in range(DEPTH):
        g = norm_gain[i]
        x = x + 0.5 * _rmsnorm(_swiglu(_rmsnorm(x, g[0]), ffn_w_gate[i, 0], ffn_w_up[i, 0],
                                       ffn_w_down[i, 0]), g[1])
        h = _rmsnorm(x, g[2])
        j = i // N_MIXERS
        if i % N_MIXERS == 0:
            m = _gla_mixer(h, gla_w_in[j], gla_w_gate_up[j], gla_b_gate[j], gla_head_gain[j], gla_w_out[j])
        else:
            m = _pool_mixer(h, pool_w[j], pool_b[j], pool_scale[j])
        x = x + _rmsnorm(m, g[3])
        x = x + 0.5 * _rmsnorm(_swiglu(_rmsnorm(x, g[4]), ffn_w_gate[i, 1], ffn_w_up[i, 1],
                                       ffn_w_down[i, 1]), g[5])
    return x
```

```python
import functools

import jax
import jax.numpy as jnp
from jax import lax
from jax.experimental import pallas as pl
from jax.experimental.pallas import tpu as pltpu

NORM_EPS = 1e-6
GLA_HEADS = 4
GLA_RANK = 16
GLA_TAU = 16.0
GLA_CHUNK = 64
POOL_WINDOWS = (2, 4, 8, 16)
POOL_HALO = 8

LANES = 128
V7X_VMEM_LIMIT_BYTES = 56 * 1024 * 1024

F32 = jnp.float32
BF16 = jnp.bfloat16


def _rms(x, gain_row):
    ms = jnp.mean(x * x, axis=-1, keepdims=True)
    return x * lax.rsqrt(ms + NORM_EPS) * gain_row


def _params(semantics):
    return pltpu.CompilerParams(dimension_semantics=semantics,
                                vmem_limit_bytes=V7X_VMEM_LIMIT_BYTES)


def _ffn_kernel(x_ref, wg_ref, wu_ref, wd_ref, g_ref, o_ref, h_ref):
    f = pl.program_id(1)

    @pl.when(f == 0)
    def _():
        h_ref[...] = _rms(x_ref[...], g_ref[0:1, :]).astype(BF16)
        o_ref[...] = jnp.zeros_like(o_ref)

    h = h_ref[...]
    a = jnp.dot(h, wg_ref[...], preferred_element_type=F32)
    b = jnp.dot(h, wu_ref[...], preferred_element_type=F32)
    c = (a * jax.nn.sigmoid(a) * b).astype(BF16)
    o_ref[...] += jnp.dot(c, wd_ref[...], preferred_element_type=F32)

    @pl.when(f == pl.num_programs(1) - 1)
    def _():
        o_ref[...] = x_ref[...] + 0.5 * _rms(o_ref[...], g_ref[1:2, :])


def _ffn(x, wg, wu, wd, gains, layer, which, *, tm, tf):
    t, d = x.shape
    d_ff = wg.shape[-1]
    grid = (t // tm, d_ff // tf)
    return pl.pallas_call(
        _ffn_kernel,
        grid=grid,
        in_specs=[
            pl.BlockSpec((tm, d), lambda i, f: (i, 0)),
            pl.BlockSpec((None, None, d, tf), lambda i, f: (layer, which, 0, f)),
            pl.BlockSpec((None, None, d, tf), lambda i, f: (layer, which, 0, f)),
            pl.BlockSpec((None, None, tf, d), lambda i, f: (layer, which, f, 0)),
            pl.BlockSpec((2, d), lambda i, f: (0, 0)),
        ],
        out_specs=pl.BlockSpec((tm, d), lambda i, f: (i, 0)),
        out_shape=jax.ShapeDtypeStruct((t, d), F32),
        scratch_shapes=[pltpu.VMEM((tm, d), BF16)],
        compiler_params=_params(("parallel", "arbitrary")),
        name="ffn",
    )(x, wg, wu, wd, gains)


def _inproj_kernel(x_ref, w_ref, wz_ref, g_ref, p_ref, z_ref, h_ref):
    n = pl.program_id(1)

    @pl.when(n == 0)
    def _():
        h = _rms(x_ref[...], g_ref[...]).astype(BF16)
        h_ref[...] = h
        z_ref[...] = jnp.dot(h, wz_ref[...], preferred_element_type=F32)

    p_ref[...] = jnp.dot(h_ref[...], w_ref[...], preferred_element_type=F32).astype(BF16)


def _inproj(x, w_main, w_z, gain, *, tm, tn):
    t, d = x.shape
    n_main = w_main.shape[1]
    return pl.pallas_call(
        _inproj_kernel,
        grid=(t // tm, n_main // tn),
        in_specs=[
            pl.BlockSpec((tm, d), lambda i, n: (i, 0)),
            pl.BlockSpec((d, tn), lambda i, n: (0, n)),
            pl.BlockSpec((d, LANES), lambda i, n: (0, 0)),
            pl.BlockSpec((1, d), lambda i, n: (0, 0)),
        ],
        out_specs=[
            pl.BlockSpec((tm, tn), lambda i, n: (i, n)),
            pl.BlockSpec((tm, LANES), lambda i, n: (i, 0)),
        ],
        out_shape=[jax.ShapeDtypeStruct((t, n_main), BF16),
                   jax.ShapeDtypeStruct((t, LANES), F32)],
        scratch_shapes=[pltpu.VMEM((tm, d), BF16)],
        compiler_params=_params(("parallel", "arbitrary")),
        name="gla_inproj",
    )(x, w_main, w_z, gain)


def _split3(x):
    hi = x.astype(BF16)
    r1 = x - hi.astype(F32)
    mid = r1.astype(BF16)
    lo = (r1 - mid.astype(F32)).astype(BF16)
    return hi, mid, lo


def _gla_direction(q_ref, k_ref, v_ref, z_ref, wup_ref, bgate_row, s_ref, o_ref, *, reverse):
    c = q_ref.shape[0]
    dk = q_ref.shape[1] // GLA_HEADS
    dv = v_ref.shape[1] // GLA_HEADS

    row = lax.broadcasted_iota(jnp.int32, (c, c), 0)
    col = lax.broadcasted_iota(jnp.int32, (c, c), 1)
    mask = (col >= row) if reverse else (col <= row)
    ones_tri = jnp.where(mask, 1.0, 0.0).astype(BF16)

    gate = jnp.dot(z_ref[...].astype(BF16), wup_ref[...], preferred_element_type=F32) + bgate_row
    log_a = jax.nn.log_sigmoid(gate) * (1.0 / GLA_TAU)
    hi, mid, lo = _split3(log_a)
    b = (jnp.dot(ones_tri, hi, preferred_element_type=F32)
         + jnp.dot(ones_tri, mid, preferred_element_type=F32)
         + jnp.dot(ones_tri, lo, preferred_element_type=F32))
    b_last = b[0:1, :] if reverse else b[c - 1:c, :]

    q = q_ref[...].astype(F32)
    k = k_ref[...].astype(F32)
    q_in = (q * jnp.exp(b) * (dk ** -0.5)).astype(BF16)
    k_in = (k * jnp.exp(-b)).astype(BF16)
    k_end = (k * jnp.exp(b_last - b)).astype(BF16)
    decay = jnp.exp(b_last)

    for h in range(GLA_HEADS):
        ks = slice(h * dk, (h + 1) * dk)
        vs = slice(h * dv, (h + 1) * dv)
        v_h = v_ref[:, vs]
        scores = lax.dot_general(q_in[:, ks], k_in[:, ks], (((1,), (1,)), ((), ())),
                                 preferred_element_type=F32)
        scores = jnp.where(mask, scores, 0.0).astype(BF16)
        state = s_ref[h]
        o = (jnp.dot(scores, v_h, preferred_element_type=F32)
             + jnp.dot(q_in[:, ks], state.astype(BF16), preferred_element_type=F32))
        o_ref[:, vs] = o.astype(o_ref.dtype)
        upd = lax.dot_general(k_end[:, ks], v_h, (((0,), (0,)), ((), ())),
                              preferred_element_type=F32)
        dcol = jnp.transpose(jnp.broadcast_to(decay[:, ks], (LANES, dk)))
        dmat = jnp.concatenate([dcol] * (dv // LANES), axis=1) if dv > LANES else dcol[:, :dv]
        s_ref[h] = state * dmat + upd


def _gla_scan_kernel(qf_ref, kf_ref, vf_ref, zf_ref, qb_ref, kb_ref, vb_ref, zb_ref,
                     wupf_ref, wupb_ref, bg_ref, of_ref, ob_ref, sf_ref, sb_ref):
    @pl.when(pl.program_id(1) == 0)
    def _():
        sf_ref[...] = jnp.zeros_like(sf_ref)
        sb_ref[...] = jnp.zeros_like(sb_ref)

    _gla_direction(qf_ref, kf_ref, vf_ref, zf_ref, wupf_ref, bg_ref[0:1, :], sf_ref, of_ref,
                   reverse=False)
    _gla_direction(qb_ref, kb_ref, vb_ref, zb_ref, wupb_ref, bg_ref[1:2, :], sb_ref, ob_ref,
                   reverse=True)


def _gla_scan(proj, z, wup_f, wup_b, b_gate, *, batch, seq, gla_k, gla_v):
    t = proj.shape[0]
    c = GLA_CHUNK
    n = seq // c
    dk = gla_k // GLA_HEADS
    dv = gla_v // GLA_HEADS
    kq = gla_v // gla_k

    def fwd(col):
        return lambda b, i: (b * n + i, col)

    def bwd(col):
        return lambda b, i: (b * n + (n - 1 - i), col)

    v_col = (2 * gla_k) // gla_v
    assert v_col * gla_v == 2 * gla_k and kq >= 1
    const = lambda b, i: (0, 0)
    in_specs = []
    for m in (fwd, bwd):
        in_specs += [
            pl.BlockSpec((c, gla_k), m(0)),
            pl.BlockSpec((c, gla_k), m(1)),
            pl.BlockSpec((c, gla_v), m(v_col)),
        ]
        in_specs.append(pl.BlockSpec((c, LANES), m(0)))
    in_specs += [
        pl.BlockSpec((LANES, gla_k), const),
        pl.BlockSpec((LANES, gla_k), const),
        pl.BlockSpec((2, gla_k), const),
    ]
    return pl.pallas_call(
        _gla_scan_kernel,
        grid=(batch, n),
        in_specs=in_specs,
        out_specs=[pl.BlockSpec((c, gla_v), fwd(0)), pl.BlockSpec((c, gla_v), bwd(0))],
        out_shape=[jax.ShapeDtypeStruct((t, gla_v), BF16)] * 2,
        scratch_shapes=[pltpu.VMEM((GLA_HEADS, dk, dv), F32)] * 2,
        compiler_params=_params(("parallel", "arbitrary")),
        name="gla_scan",
    )(proj, proj, proj, z, proj, proj, proj, z, wup_f, wup_b, b_gate)


def _gla_out_kernel(x_ref, of_ref, ob_ref, r_ref, hg_ref, w_ref, g_ref, o_ref):
    dv = hg_ref.shape[1]
    parts = []
    for h in range(GLA_HEADS):
        vs = slice(h * dv, (h + 1) * dv)
        o = of_ref[:, vs].astype(F32) + ob_ref[:, vs].astype(F32)
        o = _rms(o, hg_ref[...])
        r = r_ref[:, vs].astype(F32)
        parts.append((o * (r * jax.nn.sigmoid(r))).astype(BF16))
    gated = jnp.concatenate(parts, axis=1)
    y = jnp.dot(gated, w_ref[...], preferred_element_type=F32)
    o_ref[...] = x_ref[...] + _rms(y, g_ref[...])


def _gla_out(x, o_f, o_b, proj, head_gain, w_out, gain, *, tm, gla_k, gla_v):
    t, d = x.shape
    r_col = (2 * gla_k + gla_v) // gla_v
    assert r_col * gla_v == 2 * gla_k + gla_v
    return pl.pallas_call(
        _gla_out_kernel,
        grid=(t // tm,),
        in_specs=[
            pl.BlockSpec((tm, d), lambda i: (i, 0)),
            pl.BlockSpec((tm, gla_v), lambda i: (i, 0)),
            pl.BlockSpec((tm, gla_v), lambda i: (i, 0)),
            pl.BlockSpec((tm, gla_v), lambda i: (i, r_col)),
            pl.BlockSpec((1, gla_v // GLA_HEADS), lambda i: (0, 0)),
            pl.BlockSpec((gla_v, d), lambda i: (0, 0)),
            pl.BlockSpec((1, d), lambda i: (0, 0)),
        ],
        out_specs=pl.BlockSpec((tm, d), lambda i: (i, 0)),
        out_shape=jax.ShapeDtypeStruct((t, d), F32),
        compiler_params=_params(("parallel",)),
        name="gla_out",
    )(x, o_f, o_b, proj, head_gain, w_out, gain)


def _pool_kernel(xp_ref, x_ref, xn_ref, w_ref, g_ref, bs_ref, o_ref, *, seq):
    tm, d = x_ref.shape
    n_groups = len(POOL_WINDOWS)
    gw = d // n_groups
    pos0 = (pl.program_id(0) % (seq // tm)) * tm
    x = x_ref[...]
    h = _rms(x, g_ref[0:1, :])
    hp = jnp.where(pos0 > 0, _rms(xp_ref[...], g_ref[0:1, :]), 0.0)
    hn = jnp.where(pos0 + tm < seq, _rms(xn_ref[...], g_ref[0:1, :]), 0.0)
    ext = jnp.concatenate([hp, h, hn], axis=0)
    n_ext = tm + 2 * POOL_HALO
    tpos = pos0 + lax.broadcasted_iota(jnp.int32, (tm, gw), 0)

    def ahead(a, j):
        return pltpu.roll(a, (n_ext - j) % n_ext, axis=0)

    ys = []
    for g, w in enumerate(POOL_WINDOWS):
        cs = slice(g * gw, (g + 1) * gw)
        acc = ext[:, cs]
        if POOL_HALO - w // 2:
            acc = ahead(acc, POOL_HALO - w // 2)
        span = 1
        while span < w:
            acc = acc + ahead(acc, span)
            span *= 2
        cnt = (jnp.minimum(tpos + w // 2, seq) - jnp.maximum(tpos - w // 2, 0)).astype(F32)
        pooled = acc[0:tm, :] / cnt - h[:, cs]
        ys.append(jnp.dot(pooled.astype(BF16), w_ref[g], preferred_element_type=F32))
    y = (jnp.concatenate(ys, axis=1) + bs_ref[0:1, :]) * bs_ref[1:2, :]
    o_ref[...] = x + _rms(y, g_ref[1:2, :])


def _pool(x, w_pool, gains, bias_scale, *, tm, seq):
    t, d = x.shape
    hb = tm // POOL_HALO
    last = t // POOL_HALO - 1
    n_groups, gw, _ = w_pool.shape
    return pl.pallas_call(
        functools.partial(_pool_kernel, seq=seq),
        grid=(t // tm,),
        in_specs=[
            pl.BlockSpec((POOL_HALO, d), lambda i: (jnp.maximum(i * hb - 1, 0), 0)),
            pl.BlockSpec((tm, d), lambda i: (i, 0)),
            pl.BlockSpec((POOL_HALO, d), lambda i: (jnp.minimum((i + 1) * hb, last), 0)),
            pl.BlockSpec((n_groups, gw, gw), lambda i: (0, 0, 0)),
            pl.BlockSpec((2, d), lambda i: (0, 0)),
            pl.BlockSpec((2, d), lambda i: (0, 0)),
        ],
        out_specs=pl.BlockSpec((tm, d), lambda i: (i, 0)),
        out_shape=jax.ShapeDtypeStruct((t, d), F32),
        compiler_params=_params(("parallel",)),
        name="pool_mixer",
    )(x, x, x, w_pool, gains, bias_scale)


def _tile(n, target, quantum):
    best = None
    for cand in range(quantum, min(n, target) + 1, quantum):
        if n % cand == 0:
            best = cand
    assert best is not None, (n, target, quantum)
    return best


def kernel(x, ffn_w_gate, ffn_w_up, ffn_w_down, norm_gain, gla_w_in, gla_w_gate_up, gla_b_gate,
           gla_head_gain, gla_w_out, pool_w, pool_b, pool_scale):
    batch, seq, d = x.shape
    depth = ffn_w_gate.shape[0]
    d_ff = ffn_w_gate.shape[-1]
    gla_k = gla_w_gate_up.shape[-1]
    gla_v = gla_w_out.shape[1]
    t = batch * seq
    assert seq % GLA_CHUNK == 0

    tm = _tile(seq, 512, 16)
    tf = _tile(d_ff, 512, LANES)
    tn = _tile(2 * gla_k + 2 * gla_v, 2048, LANES)

    wg = ffn_w_gate.astype(BF16)
    wu = ffn_w_up.astype(BF16)
    wd = ffn_w_down.astype(BF16)

    xs = x.reshape(t, d)
    for i in range(depth):
        g = norm_gain[i]
        xs = _ffn(xs, wg, wu, wd, g[0:2], i, 0, tm=tm, tf=tf)
        j = i // 2
        if i % 2 == 0:
            n_main = 2 * gla_k + 2 * gla_v
            w_in = gla_w_in[j]
            w_main = w_in[:, :n_main].astype(BF16)
            w_z = jnp.pad(w_in[:, n_main:], ((0, 0), (0, LANES - 2 * GLA_RANK))).astype(BF16)
            wup = gla_w_gate_up[j].astype(BF16)
            wup_f = jnp.pad(wup[0], ((0, LANES - GLA_RANK), (0, 0)))
            wup_b = jnp.pad(wup[1], ((GLA_RANK, LANES - 2 * GLA_RANK), (0, 0)))
            proj, z = _inproj(xs, w_main, w_z, g[2:3], tm=tm, tn=tn)
            o_f, o_b = _gla_scan(proj, z, wup_f, wup_b, gla_b_gate[j], batch=batch, seq=seq,
                                 gla_k=gla_k, gla_v=gla_v)
            xs = _gla_out(xs, o_f, o_b, proj, gla_head_gain[j][None, :],
                          gla_w_out[j].astype(BF16), g[3:4], tm=tm, gla_k=gla_k, gla_v=gla_v)
        else:
            xs = _pool(xs, pool_w[j].astype(BF16), g[2:4],
                       jnp.stack([pool_b[j], pool_scale[j]]), tm=tm, seq=seq)
        xs = _ffn(xs, wg, wu, wd, g[4:6], i, 1, tm=tm, tf=tf)
    return xs.reshape(batch, seq, d)
```

```python
import functools

import jax
import jax.numpy as jnp
from jax import lax
from jax.experimental import pallas as pl
from jax.experimental.pallas import tpu as pltpu

NORM_EPS = 1e-6
GLA_HEADS = 4
GLA_RANK = 16
GLA_TAU = 16.0
GLA_CHUNK = 64
POOL_WINDOWS = (2, 4, 8, 16)
POOL_HALO = 8

LANES = 128
V7X_VMEM_LIMIT_BYTES = 56 * 1024 * 1024

F32 = jnp.float32
BF16 = jnp.bfloat16


def _rms(x, gain_row):
    ms = jnp.mean(x * x, axis=-1, keepdims=True)
    return x * lax.rsqrt(ms + NORM_EPS) * gain_row


def _params(semantics):
    return pltpu.CompilerParams(dimension_semantics=semantics,
                                vmem_limit_bytes=V7X_VMEM_LIMIT_BYTES)


def _ffn_kernel(x_ref, wg_ref, wu_ref, wd_ref, g_ref, o_ref, h_ref):
    f = pl.program_id(1)

    @pl.when(f == 0)
    def _():
        h_ref[...] = _rms(x_ref[...], g_ref[0:1, :]).astype(BF16)
        o_ref[...] = jnp.zeros_like(o_ref)

    h = h_ref[...]
    a = jnp.dot(h, wg_ref[...], preferred_element_type=F32)
    b = jnp.dot(h, wu_ref[...], preferred_element_type=F32)
    c = (a * jax.nn.sigmoid(a) * b).astype(BF16)
    o_ref[...] += jnp.dot(c, wd_ref[...], preferred_element_type=F32)

    @pl.when(f == pl.num_programs(1) - 1)
    def _():
        o_ref[...] = x_ref[...] + 0.5 * _rms(o_ref[...], g_ref[1:2, :])


def _ffn(x, wg, wu, wd, gains, layer, which, *, tm, tf):
    t, d = x.shape
    d_ff = wg.shape[-1]
    grid = (t // tm, d_ff // tf)
    return pl.pallas_call(
        _ffn_kernel,
        grid=grid,
        in_specs=[
            pl.BlockSpec((tm, d), lambda i, f: (i, 0)),
            pl.BlockSpec((None, None, d, tf), lambda i, f: (layer, which, 0, f)),
            pl.BlockSpec((None, None, d, tf), lambda i, f: (layer, which, 0, f)),
            pl.BlockSpec((None, None, tf, d), lambda i, f: (layer, which, f, 0)),
            pl.BlockSpec((2, d), lambda i, f: (0, 0)),
        ],
        out_specs=pl.BlockSpec((tm, d), lambda i, f: (i, 0)),
        out_shape=jax.ShapeDtypeStruct((t, d), F32),
        scratch_shapes=[pltpu.VMEM((tm, d), BF16)],
        compiler_params=_params(("parallel", "arbitrary")),
        name="ffn",
    )(x, wg, wu, wd, gains)


def _inproj_kernel(x_ref, w_ref, wz_ref, g_ref, p_ref, z_ref, h_ref):
    n = pl.program_id(1)

    @pl.when(n == 0)
    def _():
        h = _rms(x_ref[...], g_ref[...]).astype(BF16)
        h_ref[...] = h
        z_ref[...] = jnp.dot(h, wz_ref[...], preferred_element_type=F32)

    p_ref[...] = jnp.dot(h_ref[...], w_ref[...], preferred_element_type=F32).astype(BF16)


def _inproj(x, w_main, w_z, gain, *, tm, tn):
    t, d = x.shape
    n_main = w_main.shape[1]
    return pl.pallas_call(
        _inproj_kernel,
        grid=(t // tm, n_main // tn),
        in_specs=[
            pl.BlockSpec((tm, d), lambda i, n: (i, 0)),
            pl.BlockSpec((d, tn), lambda i, n: (0, n)),
            pl.BlockSpec((d, LANES), lambda i, n: (0, 0)),
            pl.BlockSpec((1, d), lambda i, n: (0, 0)),
        ],
        out_specs=[
            pl.BlockSpec((tm, tn), lambda i, n: (i, n)),
            pl.BlockSpec((tm, LANES), lambda i, n: (i, 0)),
        ],
        out_shape=[jax.ShapeDtypeStruct((t, n_main), BF16),
                   jax.ShapeDtypeStruct((t, LANES), F32)],
        scratch_shapes=[pltpu.VMEM((tm, d), BF16)],
        compiler_params=_params(("parallel", "arbitrary")),
        name="gla_inproj",
    )(x, w_main, w_z, gain)


def _split2(x):
    hi = x.astype(BF16)
    lo = (x - hi.astype(F32)).astype(BF16)
    return hi, lo


def _gla_direction(q_ref, k_ref, v_ref, z_ref, wup_ref, bgate_row, s_ref, o_ref, *, reverse):
    c = GLA_CHUNK
    blk, hk = q_ref.shape
    nb = blk // c
    dk = hk // GLA_HEADS
    dv = v_ref.shape[1] // GLA_HEADS
    rows = lambda j: slice(j * c, (j + 1) * c)

    row = lax.broadcasted_iota(jnp.int32, (blk, blk), 0)
    col = lax.broadcasted_iota(jnp.int32, (blk, blk), 1)
    same_chunk = (row // c) == (col // c)
    tri = (col >= row) if reverse else (col <= row)
    ones_tri = jnp.where(same_chunk, jnp.where(tri, 1.0, 0.0), 0.0).astype(BF16)
    score_mask = tri

    gate = jnp.dot(z_ref[...].astype(BF16), wup_ref[...], preferred_element_type=F32) + bgate_row
    log_a = (jnp.minimum(gate, 0.0) - jnp.log(1.0 + jnp.exp(-jnp.abs(gate)))) * (1.0 / GLA_TAU)
    hi, lo = _split2(log_a)
    b = (jnp.dot(ones_tri, hi, preferred_element_type=F32)
         + jnp.dot(ones_tri, lo, preferred_element_type=F32))
    g = [b[j * c:j * c + 1, :] if reverse else b[(j + 1) * c - 1:(j + 1) * c, :] for j in range(nb)]
    eg = [jnp.exp(gj) for gj in g]
    g_rows = jnp.concatenate([jnp.broadcast_to(gj, (c, hk)) for gj in g], axis=0)

    q = q_ref[...].astype(F32)
    k = k_ref[...].astype(F32)
    q_in = q * jnp.exp(b) * (dk ** -0.5)
    k_in = (k * jnp.exp(-b)).astype(BF16)
    k_end = k * jnp.exp(g_rows - b)
    q_in_b = q_in.astype(BF16)

    order = list(range(nb))[::-1] if reverse else list(range(nb))
    kk = {}
    keys = {}
    q_blk = [None] * nb
    run = None
    for p, i in enumerate(order):
        keys[i] = {j: kk[j].astype(BF16) for j in order[:p]}
        q_blk[i] = q_in[rows(i)] if run is None else q_in[rows(i)] * run
        for j in order[:p]:
            kk[j] = kk[j] * eg[i]
        kk[i] = k_end[rows(i)]
        run = eg[i] if run is None else run * eg[i]
    decay = run
    k_blk = jnp.concatenate([kk[j] for j in range(nb)], axis=0).astype(BF16)
    q_blk = jnp.concatenate(q_blk, axis=0).astype(BF16)
    zeros = jnp.zeros((c, dk), BF16)

    for h in range(GLA_HEADS):
        ks = slice(h * dk, (h + 1) * dk)
        vs = slice(h * dv, (h + 1) * dv)
        v_h = v_ref[:, vs]
        score_rows = []
        for i in range(nb):
            key_i = jnp.concatenate(
                [k_in[rows(i), ks] if j == i else (keys[i][j][:, ks] if j in keys[i] else zeros)
                 for j in range(nb)], axis=0)
            score_rows.append(lax.dot_general(q_in_b[rows(i), ks], key_i, (((1,), (1,)), ((), ())),
                                              preferred_element_type=F32))
        scores = jnp.where(score_mask, jnp.concatenate(score_rows, axis=0), 0.0).astype(BF16)
        state = s_ref[h]
        o = (jnp.dot(scores, v_h, preferred_element_type=F32)
             + jnp.dot(q_blk[:, ks], state.astype(BF16), preferred_element_type=F32))
        o_ref[:, vs] = o.astype(o_ref.dtype)
        upd = lax.dot_general(k_blk[:, ks], v_h, (((0,), (0,)), ((), ())),
                              preferred_element_type=F32)
        dcol = jnp.transpose(jnp.broadcast_to(decay[:, ks], (LANES, dk)))
        dmat = jnp.concatenate([dcol] * (dv // LANES), axis=1) if dv > LANES else dcol[:, :dv]
        s_ref[h] = state * dmat + upd


def _gla_scan_kernel(qf_ref, kf_ref, vf_ref, zf_ref, qb_ref, kb_ref, vb_ref, zb_ref,
                     wupf_ref, wupb_ref, bg_ref, of_ref, ob_ref, sf_ref, sb_ref):
    @pl.when(pl.program_id(1) == 0)
    def _():
        sf_ref[...] = jnp.zeros_like(sf_ref)
        sb_ref[...] = jnp.zeros_like(sb_ref)

    _gla_direction(qf_ref, kf_ref, vf_ref, zf_ref, wupf_ref, bg_ref[0:1, :], sf_ref, of_ref,
                   reverse=False)
    _gla_direction(qb_ref, kb_ref, vb_ref, zb_ref, wupb_ref, bg_ref[1:2, :], sb_ref, ob_ref,
                   reverse=True)


def _gla_scan(proj, z, wup_f, wup_b, b_gate, *, batch, seq, gla_k, gla_v, blk):
    t = proj.shape[0]
    n = seq // blk
    dk = gla_k // GLA_HEADS
    dv = gla_v // GLA_HEADS
    v_col = (2 * gla_k) // gla_v
    assert v_col * gla_v == 2 * gla_k and blk % GLA_CHUNK == 0

    def fwd(col):
        return lambda b, i: (b * n + i, col)

    def bwd(col):
        return lambda b, i: (b * n + (n - 1 - i), col)

    const = lambda b, i: (0, 0)
    in_specs = []
    for m in (fwd, bwd):
        in_specs += [
            pl.BlockSpec((blk, gla_k), m(0)),
            pl.BlockSpec((blk, gla_k), m(1)),
            pl.BlockSpec((blk, gla_v), m(v_col)),
            pl.BlockSpec((blk, LANES), m(0)),
        ]
    in_specs += [
        pl.BlockSpec((LANES, gla_k), const),
        pl.BlockSpec((LANES, gla_k), const),
        pl.BlockSpec((2, gla_k), const),
    ]
    return pl.pallas_call(
        _gla_scan_kernel,
        grid=(batch, n),
        in_specs=in_specs,
        out_specs=[pl.BlockSpec((blk, gla_v), fwd(0)), pl.BlockSpec((blk, gla_v), bwd(0))],
        out_shape=[jax.ShapeDtypeStruct((t, gla_v), BF16)] * 2,
        scratch_shapes=[pltpu.VMEM((GLA_HEADS, dk, dv), F32)] * 2,
        compiler_params=_params(("parallel", "arbitrary")),
        name="gla_scan",
    )(proj, proj, proj, z, proj, proj, proj, z, wup_f, wup_b, b_gate)


def _gla_out_kernel(x_ref, of_ref, ob_ref, r_ref, hg_ref, w_ref, g_ref, o_ref):
    dv = hg_ref.shape[1]
    parts = []
    for h in range(GLA_HEADS):
        vs = slice(h * dv, (h + 1) * dv)
        o = of_ref[:, vs].astype(F32) + ob_ref[:, vs].astype(F32)
        o = _rms(o, hg_ref[...])
        r = r_ref[:, vs].astype(F32)
        parts.append((o * (r * jax.nn.sigmoid(r))).astype(BF16))
    gated = jnp.concatenate(parts, axis=1)
    y = jnp.dot(gated, w_ref[...], preferred_element_type=F32)
    o_ref[...] = x_ref[...] + _rms(y, g_ref[...])


def _gla_out(x, o_f, o_b, proj, head_gain, w_out, gain, *, tm, gla_k, gla_v):
    t, d = x.shape
    r_col = (2 * gla_k + gla_v) // gla_v
    assert r_col * gla_v == 2 * gla_k + gla_v
    return pl.pallas_call(
        _gla_out_kernel,
        grid=(t // tm,),
        in_specs=[
            pl.BlockSpec((tm, d), lambda i: (i, 0)),
            pl.BlockSpec((tm, gla_v), lambda i: (i, 0)),
            pl.BlockSpec((tm, gla_v), lambda i: (i, 0)),
            pl.BlockSpec((tm, gla_v), lambda i: (i, r_col)),
            pl.BlockSpec((1, gla_v // GLA_HEADS), lambda i: (0, 0)),
            pl.BlockSpec((gla_v, d), lambda i: (0, 0)),
            pl.BlockSpec((1, d), lambda i: (0, 0)),
        ],
        out_specs=pl.BlockSpec((tm, d), lambda i: (i, 0)),
        out_shape=jax.ShapeDtypeStruct((t, d), F32),
        compiler_params=_params(("parallel",)),
        name="gla_out",
    )(x, o_f, o_b, proj, head_gain, w_out, gain)


def _pool_kernel(xp_ref, x_ref, xn_ref, w_ref, g_ref, bs_ref, o_ref, *, seq):
    tm, d = x_ref.shape
    n_groups = len(POOL_WINDOWS)
    gw = d // n_groups
    pos0 = (pl.program_id(0) % (seq // tm)) * tm
    x = x_ref[...]
    h = _rms(x, g_ref[0:1, :])
    hp = jnp.where(pos0 > 0, _rms(xp_ref[...], g_ref[0:1, :]), 0.0)
    hn = jnp.where(pos0 + tm < seq, _rms(xn_ref[...], g_ref[0:1, :]), 0.0)
    ext = jnp.concatenate([hp, h, hn], axis=0)
    n_ext = tm + 2 * POOL_HALO
    tpos = pos0 + lax.broadcasted_iota(jnp.int32, (tm, gw), 0)

    def ahead(a, j):
        return pltpu.roll(a, (n_ext - j) % n_ext, axis=0)

    ys = []
    for g, w in enumerate(POOL_WINDOWS):
        cs = slice(g * gw, (g + 1) * gw)
        acc = ext[:, cs]
        if POOL_HALO - w // 2:
            acc = ahead(acc, POOL_HALO - w // 2)
        span = 1
        while span < w:
            acc = acc + ahead(acc, span)
            span *= 2
        cnt = (jnp.minimum(tpos + w // 2, seq) - jnp.maximum(tpos - w // 2, 0)).astype(F32)
        pooled = acc[0:tm, :] / cnt - h[:, cs]
        ys.append(jnp.dot(pooled.astype(BF16), w_ref[g], preferred_element_type=F32))
    y = (jnp.concatenate(ys, axis=1) + bs_ref[0:1, :]) * bs_ref[1:2, :]
    o_ref[...] = x + _rms(y, g_ref[1:2, :])


def _pool(x, w_pool, gains, bias_scale, *, tm, seq):
    t, d = x.shape
    hb = tm // POOL_HALO
    last = t // POOL_HALO - 1
    n_groups, gw, _ = w_pool.shape
    return pl.pallas_call(
        functools.partial(_pool_kernel, seq=seq),
        grid=(t // tm,),
        in_specs=[
            pl.BlockSpec((POOL_HALO, d), lambda i: (jnp.maximum(i * hb - 1, 0), 0)),
            pl.BlockSpec((tm, d), lambda i: (i, 0)),
            pl.BlockSpec((POOL_HALO, d), lambda i: (jnp.minimum((i + 1) * hb, last), 0)),
            pl.BlockSpec((n_groups, gw, gw), lambda i: (0, 0, 0)),
            pl.BlockSpec((2, d), lambda i: (0, 0)),
            pl.BlockSpec((2, d), lambda i: (0, 0)),
        ],
        out_specs=pl.BlockSpec((tm, d), lambda i: (i, 0)),
        out_shape=jax.ShapeDtypeStruct((t, d), F32),
        compiler_params=_params(("parallel",)),
        name="pool_mixer",
    )(x, x, x, w_pool, gains, bias_scale)


def _tile(n, target, quantum):
    best = None
    for cand in range(quantum, min(n, target) + 1, quantum):
        if n % cand == 0:
            best = cand
    assert best is not None, (n, target, quantum)
    return best


def kernel(x, ffn_w_gate, ffn_w_up, ffn_w_down, norm_gain, gla_w_in, gla_w_gate_up, gla_b_gate,
           gla_head_gain, gla_w_out, pool_w, pool_b, pool_scale):
    batch, seq, d = x.shape
    depth = ffn_w_gate.shape[0]
    d_ff = ffn_w_gate.shape[-1]
    gla_k = gla_w_gate_up.shape[-1]
    gla_v = gla_w_out.shape[1]
    t = batch * seq
    assert seq % GLA_CHUNK == 0

    tm = _tile(seq, 512, 16)
    tf = _tile(d_ff, 512, LANES)
    tn = _tile(2 * gla_k + 2 * gla_v, 2048, LANES)
    blk = _tile(seq, 256, GLA_CHUNK)

    wg = ffn_w_gate.astype(BF16)
    wu = ffn_w_up.astype(BF16)
    wd = ffn_w_down.astype(BF16)

    xs = x.reshape(t, d)
    for i in range(depth):
        g = norm_gain[i]
        xs = _ffn(xs, wg, wu, wd, g[0:2], i, 0, tm=tm, tf=tf)
        j = i // 2
        if i % 2 == 0:
            n_main = 2 * gla_k + 2 * gla_v
            w_in = gla_w_in[j]
            w_main = w_in[:, :n_main].astype(BF16)
            w_z = jnp.pad(w_in[:, n_main:], ((0, 0), (0, LANES - 2 * GLA_RANK))).astype(BF16)
            wup = gla_w_gate_up[j].astype(BF16)
            wup_f = jnp.pad(wup[0], ((0, LANES - GLA_RANK), (0, 0)))
            wup_b = jnp.pad(wup[1], ((GLA_RANK, LANES - 2 * GLA_RANK), (0, 0)))
            proj, z = _inproj(xs, w_main, w_z, g[2:3], tm=tm, tn=tn)
            o_f, o_b = _gla_scan(proj, z, wup_f, wup_b, gla_b_gate[j], batch=batch, seq=seq,
                                 gla_k=gla_k, gla_v=gla_v, blk=blk)
            xs = _gla_out(xs, o_f, o_b, proj, gla_head_gain[j][None, :],
                          gla_w_out[j].astype(BF16), g[3:4], tm=tm, gla_k=gla_k, gla_v=gla_v)
        else:
            xs = _pool(xs, pool_w[j].astype(BF16), g[2:4],
                       jnp.stack([pool_b[j], pool_scale[j]]), tm=tm, seq=seq)
        xs = _ffn(xs, wg, wu, wd, g[4:6], i, 1, tm=tm, tf=tf)
    return xs.reshape(batch, seq, d)
```

```python
import functools

import jax
import jax.numpy as jnp
from jax import lax
from jax.experimental import pallas as pl
from jax.experimental.pallas import tpu as pltpu

NORM_EPS = 1e-6
GLA_HEADS = 4
GLA_RANK = 16
GLA_TAU = 16.0
GLA_CHUNK = 64
POOL_WINDOWS = (2, 4, 8, 16)
POOL_HALO = 8

LANES = 128
V7X_VMEM_LIMIT_BYTES = 62 * 1024 * 1024

F32 = jnp.float32
BF16 = jnp.bfloat16


def _rms(x, gain_row):
    ms = jnp.mean(x * x, axis=-1, keepdims=True)
    return x * lax.rsqrt(ms + NORM_EPS) * gain_row


def _params(semantics):
    return pltpu.CompilerParams(dimension_semantics=semantics,
                                vmem_limit_bytes=V7X_VMEM_LIMIT_BYTES)


def _ffn_kernel(x_ref, wg_ref, wu_ref, wd_ref, g_ref, o_ref, h_ref):
    f = pl.program_id(1)

    @pl.when(f == 0)
    def _():
        h_ref[...] = _rms(x_ref[...], g_ref[0:1, :]).astype(BF16)
        o_ref[...] = jnp.zeros_like(o_ref)

    h = h_ref[...]
    a = jnp.dot(h, wg_ref[...], preferred_element_type=F32)
    b = jnp.dot(h, wu_ref[...], preferred_element_type=F32)
    c = (a * jax.nn.sigmoid(a) * b).astype(BF16)
    o_ref[...] += jnp.dot(c, wd_ref[...], preferred_element_type=F32)

    @pl.when(f == pl.num_programs(1) - 1)
    def _():
        o_ref[...] = x_ref[...] + 0.5 * _rms(o_ref[...], g_ref[1:2, :])


def _ffn(x, wg, wu, wd, gains, layer, which, *, tm, tf):
    t, d = x.shape
    d_ff = wg.shape[-1]
    grid = (t // tm, d_ff // tf)
    return pl.pallas_call(
        _ffn_kernel,
        grid=grid,
        in_specs=[
            pl.BlockSpec((tm, d), lambda i, f: (i, 0)),
            pl.BlockSpec((None, None, d, tf), lambda i, f: (layer, which, 0, f)),
            pl.BlockSpec((None, None, d, tf), lambda i, f: (layer, which, 0, f)),
            pl.BlockSpec((None, None, tf, d), lambda i, f: (layer, which, f, 0)),
            pl.BlockSpec((2, d), lambda i, f: (0, 0)),
        ],
        out_specs=pl.BlockSpec((tm, d), lambda i, f: (i, 0)),
        out_shape=jax.ShapeDtypeStruct((t, d), F32),
        scratch_shapes=[pltpu.VMEM((tm, d), BF16)],
        compiler_params=_params(("parallel", "arbitrary")),
        name="ffn",
    )(x, wg, wu, wd, gains)


def _inproj_kernel(x_ref, w_ref, wz_ref, g_ref, p_ref, z_ref, h_ref):
    n = pl.program_id(1)

    @pl.when(n == 0)
    def _():
        h = _rms(x_ref[...], g_ref[...]).astype(BF16)
        h_ref[...] = h
        z_ref[...] = jnp.dot(h, wz_ref[...], preferred_element_type=F32)

    p_ref[...] = jnp.dot(h_ref[...], w_ref[...], preferred_element_type=F32).astype(BF16)


def _inproj(x, w_main, w_z, gain, *, tm, tn):
    t, d = x.shape
    n_main = w_main.shape[1]
    return pl.pallas_call(
        _inproj_kernel,
        grid=(t // tm, n_main // tn),
        in_specs=[
            pl.BlockSpec((tm, d), lambda i, n: (i, 0)),
            pl.BlockSpec((d, tn), lambda i, n: (0, n)),
            pl.BlockSpec((d, LANES), lambda i, n: (0, 0)),
            pl.BlockSpec((1, d), lambda i, n: (0, 0)),
        ],
        out_specs=[
            pl.BlockSpec((tm, tn), lambda i, n: (i, n)),
            pl.BlockSpec((tm, LANES), lambda i, n: (i, 0)),
        ],
        out_shape=[jax.ShapeDtypeStruct((t, n_main), BF16),
                   jax.ShapeDtypeStruct((t, LANES), F32)],
        scratch_shapes=[pltpu.VMEM((tm, d), BF16)],
        compiler_params=_params(("parallel", "arbitrary")),
        name="gla_inproj",
    )(x, w_main, w_z, gain)


def _split2(x):
    hi = x.astype(BF16)
    lo = (x - hi.astype(F32)).astype(BF16)
    return hi, lo


def _gla_direction(q_ref, k_ref, v_ref, z_ref, wup_ref, bgate_row, s_ref, o_ref, *, reverse):
    c = GLA_CHUNK
    blk, hk = q_ref.shape
    nb = blk // c
    dk = hk // GLA_HEADS
    dv = v_ref.shape[1] // GLA_HEADS
    rows = lambda j: slice(j * c, (j + 1) * c)

    row = lax.broadcasted_iota(jnp.int32, (blk, blk), 0)
    col = lax.broadcasted_iota(jnp.int32, (blk, blk), 1)
    same_chunk = (row // c) == (col // c)
    tri = (col >= row) if reverse else (col <= row)
    ones_tri = jnp.where(same_chunk, jnp.where(tri, 1.0, 0.0), 0.0).astype(BF16)
    score_mask = tri

    gate = jnp.dot(z_ref[...].astype(BF16), wup_ref[...], preferred_element_type=F32) + bgate_row
    log_a = (jnp.minimum(gate, 0.0) - jnp.log(1.0 + jnp.exp(-jnp.abs(gate)))) * (1.0 / GLA_TAU)
    hi, lo = _split2(log_a)
    b = (jnp.dot(ones_tri, hi, preferred_element_type=F32)
         + jnp.dot(ones_tri, lo, preferred_element_type=F32))
    g = [b[j * c:j * c + 1, :] if reverse else b[(j + 1) * c - 1:(j + 1) * c, :] for j in range(nb)]
    eg = [jnp.exp(gj) for gj in g]
    g_rows = jnp.concatenate([jnp.broadcast_to(gj, (c, hk)) for gj in g], axis=0)

    q = q_ref[...].astype(F32)
    k = k_ref[...].astype(F32)
    q_in = q * jnp.exp(b) * (dk ** -0.5)
    k_in = (k * jnp.exp(-b)).astype(BF16)
    k_end = k * jnp.exp(g_rows - b)
    q_in_b = q_in.astype(BF16)

    order = list(range(nb))[::-1] if reverse else list(range(nb))
    kk = {}
    keys = {}
    q_blk = [None] * nb
    run = None
    for p, i in enumerate(order):
        keys[i] = {j: kk[j].astype(BF16) for j in order[:p]}
        q_blk[i] = q_in[rows(i)] if run is None else q_in[rows(i)] * run
        for j in order[:p]:
            kk[j] = kk[j] * eg[i]
        kk[i] = k_end[rows(i)]
        run = eg[i] if run is None else run * eg[i]
    decay = run
    k_blk = jnp.concatenate([kk[j] for j in range(nb)], axis=0).astype(BF16)
    q_blk = jnp.concatenate(q_blk, axis=0).astype(BF16)
    zeros = jnp.zeros((c, dk), BF16)

    for h in range(GLA_HEADS):
        ks = slice(h * dk, (h + 1) * dk)
        vs = slice(h * dv, (h + 1) * dv)
        v_h = v_ref[:, vs]
        score_rows = []
        for i in range(nb):
            key_i = jnp.concatenate(
                [k_in[rows(i), ks] if j == i else (keys[i][j][:, ks] if j in keys[i] else zeros)
                 for j in range(nb)], axis=0)
            score_rows.append(lax.dot_general(q_in_b[rows(i), ks], key_i, (((1,), (1,)), ((), ())),
                                              preferred_element_type=F32))
        scores = jnp.where(score_mask, jnp.concatenate(score_rows, axis=0), 0.0).astype(BF16)
        state = s_ref[h]
        o = (jnp.dot(scores, v_h, preferred_element_type=F32)
             + jnp.dot(q_blk[:, ks], state.astype(BF16), preferred_element_type=F32))
        o_ref[:, vs] = o.astype(o_ref.dtype)
        upd = lax.dot_general(k_blk[:, ks], v_h, (((0,), (0,)), ((), ())),
                              preferred_element_type=F32)
        dcol = jnp.transpose(jnp.broadcast_to(decay[:, ks], (LANES, dk)))
        dmat = jnp.concatenate([dcol] * (dv // LANES), axis=1) if dv > LANES else dcol[:, :dv]
        s_ref[h] = state * dmat + upd


def _gla_scan_kernel(qf_ref, kf_ref, vf_ref, zf_ref, qb_ref, kb_ref, vb_ref, zb_ref,
                     wupf_ref, wupb_ref, bg_ref, of_ref, ob_ref, sf_ref, sb_ref):
    @pl.when(pl.program_id(1) == 0)
    def _():
        sf_ref[...] = jnp.zeros_like(sf_ref)
        sb_ref[...] = jnp.zeros_like(sb_ref)

    _gla_direction(qf_ref, kf_ref, vf_ref, zf_ref, wupf_ref, bg_ref[0:1, :], sf_ref, of_ref,
                   reverse=False)
    _gla_direction(qb_ref, kb_ref, vb_ref, zb_ref, wupb_ref, bg_ref[1:2, :], sb_ref, ob_ref,
                   reverse=True)


def _gla_scan(proj, z, wup_f, wup_b, b_gate, *, batch, seq, gla_k, gla_v, blk):
    t = proj.shape[0]
    n = seq // blk
    dk = gla_k // GLA_HEADS
    dv = gla_v // GLA_HEADS
    v_col = (2 * gla_k) // gla_v
    assert v_col * gla_v == 2 * gla_k and blk % GLA_CHUNK == 0

    def fwd(col):
        return lambda b, i: (b * n + i, col)

    def bwd(col):
        return lambda b, i: (b * n + (n - 1 - i), col)

    const = lambda b, i: (0, 0)
    in_specs = []
    for m in (fwd, bwd):
        in_specs += [
            pl.BlockSpec((blk, gla_k), m(0)),
            pl.BlockSpec((blk, gla_k), m(1)),
            pl.BlockSpec((blk, gla_v), m(v_col)),
            pl.BlockSpec((blk, LANES), m(0)),
        ]
    in_specs += [
        pl.BlockSpec((LANES, gla_k), const),
        pl.BlockSpec((LANES, gla_k), const),
        pl.BlockSpec((2, gla_k), const),
    ]
    return pl.pallas_call(
        _gla_scan_kernel,
        grid=(batch, n),
        in_specs=in_specs,
        out_specs=[pl.BlockSpec((blk, gla_v), fwd(0)), pl.BlockSpec((blk, gla_v), bwd(0))],
        out_shape=[jax.ShapeDtypeStruct((t, gla_v), BF16)] * 2,
        scratch_shapes=[pltpu.VMEM((GLA_HEADS, dk, dv), F32)] * 2,
        compiler_params=_params(("parallel", "arbitrary")),
        name="gla_scan",
    )(proj, proj, proj, z, proj, proj, proj, z, wup_f, wup_b, b_gate)


def _gla_out_kernel(x_ref, of_ref, ob_ref, r_ref, hg_ref, w_ref, g_ref, o_ref):
    dv = hg_ref.shape[1]
    parts = []
    for h in range(GLA_HEADS):
        vs = slice(h * dv, (h + 1) * dv)
        o = of_ref[:, vs].astype(F32) + ob_ref[:, vs].astype(F32)
        o = _rms(o, hg_ref[...])
        r = r_ref[:, vs].astype(F32)
        parts.append((o * (r * jax.nn.sigmoid(r))).astype(BF16))
    gated = jnp.concatenate(parts, axis=1)
    y = jnp.dot(gated, w_ref[...], preferred_element_type=F32)
    o_ref[...] = x_ref[...] + _rms(y, g_ref[...])


def _gla_out(x, o_f, o_b, proj, head_gain, w_out, gain, *, tm, gla_k, gla_v):
    t, d = x.shape
    r_col = (2 * gla_k + gla_v) // gla_v
    assert r_col * gla_v == 2 * gla_k + gla_v
    return pl.pallas_call(
        _gla_out_kernel,
        grid=(t // tm,),
        in_specs=[
            pl.BlockSpec((tm, d), lambda i: (i, 0)),
            pl.BlockSpec((tm, gla_v), lambda i: (i, 0)),
            pl.BlockSpec((tm, gla_v), lambda i: (i, 0)),
            pl.BlockSpec((tm, gla_v), lambda i: (i, r_col)),
            pl.BlockSpec((1, gla_v // GLA_HEADS), lambda i: (0, 0)),
            pl.BlockSpec((gla_v, d), lambda i: (0, 0)),
            pl.BlockSpec((1, d), lambda i: (0, 0)),
        ],
        out_specs=pl.BlockSpec((tm, d), lambda i: (i, 0)),
        out_shape=jax.ShapeDtypeStruct((t, d), F32),
        compiler_params=_params(("parallel",)),
        name="gla_out",
    )(x, o_f, o_b, proj, head_gain, w_out, gain)


def _pool_kernel(xp_ref, x_ref, xn_ref, w_ref, g_ref, bs_ref, o_ref, *, seq):
    tm, d = x_ref.shape
    n_groups = len(POOL_WINDOWS)
    gw = d // n_groups
    pos0 = (pl.program_id(0) % (seq // tm)) * tm
    x = x_ref[...]
    h = _rms(x, g_ref[0:1, :])
    hp = jnp.where(pos0 > 0, _rms(xp_ref[...], g_ref[0:1, :]), 0.0)
    hn = jnp.where(pos0 + tm < seq, _rms(xn_ref[...], g_ref[0:1, :]), 0.0)
    ext = jnp.concatenate([hp, h, hn], axis=0)
    n_ext = tm + 2 * POOL_HALO
    tpos = pos0 + lax.broadcasted_iota(jnp.int32, (tm, gw), 0)

    def ahead(a, j):
        return pltpu.roll(a, (n_ext - j) % n_ext, axis=0)

    ys = []
    for g, w in enumerate(POOL_WINDOWS):
        cs = slice(g * gw, (g + 1) * gw)
        acc = ext[:, cs]
        if POOL_HALO - w // 2:
            acc = ahead(acc, POOL_HALO - w // 2)
        span = 1
        while span < w:
            acc = acc + ahead(acc, span)
            span *= 2
        cnt = (jnp.minimum(tpos + w // 2, seq) - jnp.maximum(tpos - w // 2, 0)).astype(F32)
        pooled = acc[0:tm, :] / cnt - h[:, cs]
        ys.append(jnp.dot(pooled.astype(BF16), w_ref[g], preferred_element_type=F32))
    y = (jnp.concatenate(ys, axis=1) + bs_ref[0:1, :]) * bs_ref[1:2, :]
    o_ref[...] = x + _rms(y, g_ref[1:2, :])


def _pool(x, w_pool, gains, bias_scale, *, tm, seq):
    t, d = x.shape
    hb = tm // POOL_HALO
    last = t // POOL_HALO - 1
    n_groups, gw, _ = w_pool.shape
    return pl.pallas_call(
        functools.partial(_pool_kernel, seq=seq),
        grid=(t // tm,),
        in_specs=[
            pl.BlockSpec((POOL_HALO, d), lambda i: (jnp.maximum(i * hb - 1, 0), 0)),
            pl.BlockSpec((tm, d), lambda i: (i, 0)),
            pl.BlockSpec((POOL_HALO, d), lambda i: (jnp.minimum((i + 1) * hb, last), 0)),
            pl.BlockSpec((n_groups, gw, gw), lambda i: (0, 0, 0)),
            pl.BlockSpec((2, d), lambda i: (0, 0)),
            pl.BlockSpec((2, d), lambda i: (0, 0)),
        ],
        out_specs=pl.BlockSpec((tm, d), lambda i: (i, 0)),
        out_shape=jax.ShapeDtypeStruct((t, d), F32),
        compiler_params=_params(("parallel",)),
        name="pool_mixer",
    )(x, x, x, w_pool, gains, bias_scale)


def _tile(n, target, quantum):
    best = None
    for cand in range(quantum, min(n, target) + 1, quantum):
        if n % cand == 0:
            best = cand
    assert best is not None, (n, target, quantum)
    return best


def kernel(x, ffn_w_gate, ffn_w_up, ffn_w_down, norm_gain, gla_w_in, gla_w_gate_up, gla_b_gate,
           gla_head_gain, gla_w_out, pool_w, pool_b, pool_scale):
    batch, seq, d = x.shape
    depth = ffn_w_gate.shape[0]
    d_ff = ffn_w_gate.shape[-1]
    gla_k = gla_w_gate_up.shape[-1]
    gla_v = gla_w_out.shape[1]
    t = batch * seq
    assert seq % GLA_CHUNK == 0

    tm = _tile(seq, 512, 16)
    tm_ffn = _tile(seq, 1024, 16)
    tf = _tile(d_ff, 512, LANES)
    tn = _tile(2 * gla_k + 2 * gla_v, 2048, LANES)
    blk = _tile(seq, 256, GLA_CHUNK)

    wg = ffn_w_gate.astype(BF16)
    wu = ffn_w_up.astype(BF16)
    wd = ffn_w_down.astype(BF16)

    xs = x.reshape(t, d)
    for i in range(depth):
        g = norm_gain[i]
        xs = _ffn(xs, wg, wu, wd, g[0:2], i, 0, tm=tm_ffn, tf=tf)
        j = i // 2
        if i % 2 == 0:
            n_main = 2 * gla_k + 2 * gla_v
            w_in = gla_w_in[j]
            w_main = w_in[:, :n_main].astype(BF16)
            w_z = jnp.pad(w_in[:, n_main:], ((0, 0), (0, LANES - 2 * GLA_RANK))).astype(BF16)
            wup = gla_w_gate_up[j].astype(BF16)
            wup_f = jnp.pad(wup[0], ((0, LANES - GLA_RANK), (0, 0)))
            wup_b = jnp.pad(wup[1], ((GLA_RANK, LANES - 2 * GLA_RANK), (0, 0)))
            proj, z = _inproj(xs, w_main, w_z, g[2:3], tm=tm, tn=tn)
            o_f, o_b = _gla_scan(proj, z, wup_f, wup_b, gla_b_gate[j], batch=batch, seq=seq,
                                 gla_k=gla_k, gla_v=gla_v, blk=blk)
            xs = _gla_out(xs, o_f, o_b, proj, gla_head_gain[j][None, :],
                          gla_w_out[j].astype(BF16), g[3:4], tm=tm, gla_k=gla_k, gla_v=gla_v)
        else:
            xs = _pool(xs, pool_w[j].astype(BF16), g[2:4],
                       jnp.stack([pool_b[j], pool_scale[j]]), tm=tm, seq=seq)
        xs = _ffn(xs, wg, wu, wd, g[4:6], i, 1, tm=tm_ffn, tf=tf)
    return xs.reshape(batch, seq, d)
```

```python
import functools

import jax
import jax.numpy as jnp
from jax import lax
from jax.experimental import pallas as pl
from jax.experimental.pallas import tpu as pltpu

NORM_EPS = 1e-6
GLA_HEADS = 4
GLA_RANK = 16
GLA_TAU = 16.0
GLA_CHUNK = 64
POOL_WINDOWS = (2, 4, 8, 16)
POOL_HALO = 8

LANES = 128
V7X_VMEM_LIMIT_BYTES = 62 * 1024 * 1024

F32 = jnp.float32
BF16 = jnp.bfloat16


def _rms(x, gain_row):
    ms = jnp.mean(x * x, axis=-1, keepdims=True)
    return x * lax.rsqrt(ms + NORM_EPS) * gain_row


def _params(semantics):
    return pltpu.CompilerParams(dimension_semantics=semantics,
                                vmem_limit_bytes=V7X_VMEM_LIMIT_BYTES)


FFN_ROW_SPLIT = 4


def _ffn_kernel(x_ref, wg_ref, wu_ref, wd_ref, g_ref, o_ref, h_ref):
    f = pl.program_id(1)
    last = pl.num_programs(1) - 1
    tm = x_ref.shape[0]
    rows = tm // FFN_ROW_SPLIT

    def swiglu(rs):
        h = h_ref[rs, :]
        a = jnp.dot(h, wg_ref[...], preferred_element_type=F32)
        b = jnp.dot(h, wu_ref[...], preferred_element_type=F32)
        c = (a * jax.nn.sigmoid(a) * b).astype(BF16)
        return jnp.dot(c, wd_ref[...], preferred_element_type=F32)

    @pl.when(f == 0)
    def _():
        for q in range(FFN_ROW_SPLIT):
            rs = slice(q * rows, (q + 1) * rows)
            h_ref[rs, :] = _rms(x_ref[rs, :], g_ref[0:1, :]).astype(BF16)
            o_ref[rs, :] = swiglu(rs)

    @pl.when((f > 0) & (f < last))
    def _():
        o_ref[...] += swiglu(slice(None))

    @pl.when(f == last)
    def _():
        for q in range(FFN_ROW_SPLIT):
            rs = slice(q * rows, (q + 1) * rows)
            y = o_ref[rs, :] + swiglu(rs)
            o_ref[rs, :] = x_ref[rs, :] + 0.5 * _rms(y, g_ref[1:2, :])


def _ffn(x, wg, wu, wd, gains, layer, which, *, tm, tf):
    t, d = x.shape
    d_ff = wg.shape[-1]
    grid = (t // tm, d_ff // tf)
    assert grid[1] >= 2 and tm % (16 * FFN_ROW_SPLIT) == 0
    return pl.pallas_call(
        _ffn_kernel,
        grid=grid,
        in_specs=[
            pl.BlockSpec((tm, d), lambda i, f: (i, 0)),
            pl.BlockSpec((None, None, d, tf), lambda i, f: (layer, which, 0, f)),
            pl.BlockSpec((None, None, d, tf), lambda i, f: (layer, which, 0, f)),
            pl.BlockSpec((None, None, tf, d), lambda i, f: (layer, which, f, 0)),
            pl.BlockSpec((2, d), lambda i, f: (0, 0)),
        ],
        out_specs=pl.BlockSpec((tm, d), lambda i, f: (i, 0)),
        out_shape=jax.ShapeDtypeStruct((t, d), F32),
        scratch_shapes=[pltpu.VMEM((tm, d), BF16)],
        compiler_params=_params(("parallel", "arbitrary")),
        name="ffn",
    )(x, wg, wu, wd, gains)


def _inproj_kernel(x_ref, w_ref, wz_ref, g_ref, p_ref, z_ref, h_ref):
    n = pl.program_id(1)

    @pl.when(n == 0)
    def _():
        h = _rms(x_ref[...], g_ref[...]).astype(BF16)
        h_ref[...] = h
        z_ref[...] = jnp.dot(h, wz_ref[...], preferred_element_type=F32)

    p_ref[...] = jnp.dot(h_ref[...], w_ref[...], preferred_element_type=F32).astype(BF16)


def _inproj(x, w_main, w_z, gain, *, tm, tn):
    t, d = x.shape
    n_main = w_main.shape[1]
    return pl.pallas_call(
        _inproj_kernel,
        grid=(t // tm, n_main // tn),
        in_specs=[
            pl.BlockSpec((tm, d), lambda i, n: (i, 0)),
            pl.BlockSpec((d, tn), lambda i, n: (0, n)),
            pl.BlockSpec((d, LANES), lambda i, n: (0, 0)),
            pl.BlockSpec((1, d), lambda i, n: (0, 0)),
        ],
        out_specs=[
            pl.BlockSpec((tm, tn), lambda i, n: (i, n)),
            pl.BlockSpec((tm, LANES), lambda i, n: (i, 0)),
        ],
        out_shape=[jax.ShapeDtypeStruct((t, n_main), BF16),
                   jax.ShapeDtypeStruct((t, LANES), F32)],
        scratch_shapes=[pltpu.VMEM((tm, d), BF16)],
        compiler_params=_params(("parallel", "arbitrary")),
        name="gla_inproj",
    )(x, w_main, w_z, gain)


def _split2(x):
    hi = x.astype(BF16)
    lo = (x - hi.astype(F32)).astype(BF16)
    return hi, lo


def _gla_direction(q_ref, k_ref, v_ref, z_ref, wup_ref, bgate_row, s_ref, o_ref, *, reverse):
    c = GLA_CHUNK
    blk, hk = q_ref.shape
    nb = blk // c
    dk = hk // GLA_HEADS
    dv = v_ref.shape[1] // GLA_HEADS
    rows = lambda j: slice(j * c, (j + 1) * c)

    row = lax.broadcasted_iota(jnp.int32, (blk, blk), 0)
    col = lax.broadcasted_iota(jnp.int32, (blk, blk), 1)
    same_chunk = (row // c) == (col // c)
    tri = (col >= row) if reverse else (col <= row)
    ones_tri = jnp.where(same_chunk, jnp.where(tri, 1.0, 0.0), 0.0).astype(BF16)
    score_mask = tri

    gate = jnp.dot(z_ref[...].astype(BF16), wup_ref[...], preferred_element_type=F32) + bgate_row
    log_a = (jnp.minimum(gate, 0.0) - jnp.log(1.0 + jnp.exp(-jnp.abs(gate)))) * (1.0 / GLA_TAU)
    hi, lo = _split2(log_a)
    b = (jnp.dot(ones_tri, hi, preferred_element_type=F32)
         + jnp.dot(ones_tri, lo, preferred_element_type=F32))
    g = [b[j * c:j * c + 1, :] if reverse else b[(j + 1) * c - 1:(j + 1) * c, :] for j in range(nb)]
    eg = [jnp.exp(gj) for gj in g]
    g_rows = jnp.concatenate([jnp.broadcast_to(gj, (c, hk)) for gj in g], axis=0)

    q = q_ref[...].astype(F32)
    k = k_ref[...].astype(F32)
    q_in = q * jnp.exp(b) * (dk ** -0.5)
    k_in = (k * jnp.exp(-b)).astype(BF16)
    k_end = k * jnp.exp(g_rows - b)
    q_in_b = q_in.astype(BF16)

    order = list(range(nb))[::-1] if reverse else list(range(nb))
    kk = {}
    keys = {}
    q_blk = [None] * nb
    run = None
    for p, i in enumerate(order):
        keys[i] = {j: kk[j].astype(BF16) for j in order[:p]}
        q_blk[i] = q_in[rows(i)] if run is None else q_in[rows(i)] * run
        for j in order[:p]:
            kk[j] = kk[j] * eg[i]
        kk[i] = k_end[rows(i)]
        run = eg[i] if run is None else run * eg[i]
    decay = run
    k_blk = jnp.concatenate([kk[j] for j in range(nb)], axis=0).astype(BF16)
    q_blk = jnp.concatenate(q_blk, axis=0).astype(BF16)
    zeros = jnp.zeros((c, dk), BF16)

    for h in range(GLA_HEADS):
        ks = slice(h * dk, (h + 1) * dk)
        vs = slice(h * dv, (h + 1) * dv)
        v_h = v_ref[:, vs]
        score_rows = []
        for i in range(nb):
            key_i = jnp.concatenate(
                [k_in[rows(i), ks] if j == i else (keys[i][j][:, ks] if j in keys[i] else zeros)
                 for j in range(nb)], axis=0)
            score_rows.append(lax.dot_general(q_in_b[rows(i), ks], key_i, (((1,), (1,)), ((), ())),
                                              preferred_element_type=F32))
        scores = jnp.where(score_mask, jnp.concatenate(score_rows, axis=0), 0.0).astype(BF16)
        state = s_ref[h]
        o = (jnp.dot(scores, v_h, preferred_element_type=F32)
             + jnp.dot(q_blk[:, ks], state.astype(BF16), preferred_element_type=F32))
        o_ref[:, vs] = o.astype(o_ref.dtype)
        upd = lax.dot_general(k_blk[:, ks], v_h, (((0,), (0,)), ((), ())),
                              preferred_element_type=F32)
        dcol = jnp.transpose(jnp.broadcast_to(decay[:, ks], (LANES, dk)))
        dmat = jnp.concatenate([dcol] * (dv // LANES), axis=1) if dv > LANES else dcol[:, :dv]
        s_ref[h] = state * dmat + upd


def _gla_scan_kernel(qf_ref, kf_ref, vf_ref, zf_ref, qb_ref, kb_ref, vb_ref, zb_ref,
                     wupf_ref, wupb_ref, bg_ref, of_ref, ob_ref, sf_ref, sb_ref):
    @pl.when(pl.program_id(1) == 0)
    def _():
        sf_ref[...] = jnp.zeros_like(sf_ref)
        sb_ref[...] = jnp.zeros_like(sb_ref)

    _gla_direction(qf_ref, kf_ref, vf_ref, zf_ref, wupf_ref, bg_ref[0:1, :], sf_ref, of_ref,
                   reverse=False)
    _gla_direction(qb_ref, kb_ref, vb_ref, zb_ref, wupb_ref, bg_ref[1:2, :], sb_ref, ob_ref,
                   reverse=True)


def _gla_scan(proj, z, wup_f, wup_b, b_gate, *, batch, seq, gla_k, gla_v, blk):
    t = proj.shape[0]
    n = seq // blk
    dk = gla_k // GLA_HEADS
    dv = gla_v // GLA_HEADS
    v_col = (2 * gla_k) // gla_v
    assert v_col * gla_v == 2 * gla_k and blk % GLA_CHUNK == 0

    def fwd(col):
        return lambda b, i: (b * n + i, col)

    def bwd(col):
        return lambda b, i: (b * n + (n - 1 - i), col)

    const = lambda b, i: (0, 0)
    in_specs = []
    for m in (fwd, bwd):
        in_specs += [
            pl.BlockSpec((blk, gla_k), m(0)),
            pl.BlockSpec((blk, gla_k), m(1)),
            pl.BlockSpec((blk, gla_v), m(v_col)),
            pl.BlockSpec((blk, LANES), m(0)),
        ]
    in_specs += [
        pl.BlockSpec((LANES, gla_k), const),
        pl.BlockSpec((LANES, gla_k), const),
        pl.BlockSpec((2, gla_k), const),
    ]
    return pl.pallas_call(
        _gla_scan_kernel,
        grid=(batch, n),
        in_specs=in_specs,
        out_specs=[pl.BlockSpec((blk, gla_v), fwd(0)), pl.BlockSpec((blk, gla_v), bwd(0))],
        out_shape=[jax.ShapeDtypeStruct((t, gla_v), BF16)] * 2,
        scratch_shapes=[pltpu.VMEM((GLA_HEADS, dk, dv), F32)] * 2,
        compiler_params=_params(("parallel", "arbitrary")),
        name="gla_scan",
    )(proj, proj, proj, z, proj, proj, proj, z, wup_f, wup_b, b_gate)


def _gla_out_kernel(x_ref, of_ref, ob_ref, r_ref, hg_ref, w_ref, g_ref, o_ref):
    dv = hg_ref.shape[1]
    parts = []
    for h in range(GLA_HEADS):
        vs = slice(h * dv, (h + 1) * dv)
        o = of_ref[:, vs].astype(F32) + ob_ref[:, vs].astype(F32)
        o = _rms(o, hg_ref[...])
        r = r_ref[:, vs].astype(F32)
        parts.append((o * (r * jax.nn.sigmoid(r))).astype(BF16))
    gated = jnp.concatenate(parts, axis=1)
    y = jnp.dot(gated, w_ref[...], preferred_element_type=F32)
    o_ref[...] = x_ref[...] + _rms(y, g_ref[...])


def _gla_out(x, o_f, o_b, proj, head_gain, w_out, gain, *, tm, gla_k, gla_v):
    t, d = x.shape
    r_col = (2 * gla_k + gla_v) // gla_v
    assert r_col * gla_v == 2 * gla_k + gla_v
    return pl.pallas_call(
        _gla_out_kernel,
        grid=(t // tm,),
        in_specs=[
            pl.BlockSpec((tm, d), lambda i: (i, 0)),
            pl.BlockSpec((tm, gla_v), lambda i: (i, 0)),
            pl.BlockSpec((tm, gla_v), lambda i: (i, 0)),
            pl.BlockSpec((tm, gla_v), lambda i: (i, r_col)),
            pl.BlockSpec((1, gla_v // GLA_HEADS), lambda i: (0, 0)),
            pl.BlockSpec((gla_v, d), lambda i: (0, 0)),
            pl.BlockSpec((1, d), lambda i: (0, 0)),
        ],
        out_specs=pl.BlockSpec((tm, d), lambda i: (i, 0)),
        out_shape=jax.ShapeDtypeStruct((t, d), F32),
        compiler_params=_params(("parallel",)),
        name="gla_out",
    )(x, o_f, o_b, proj, head_gain, w_out, gain)


def _pool_kernel(xp_ref, x_ref, xn_ref, w_ref, g_ref, bs_ref, o_ref, *, seq):
    tm, d = x_ref.shape
    n_groups = len(POOL_WINDOWS)
    gw = d // n_groups
    pos0 = (pl.program_id(0) % (seq // tm)) * tm
    x = x_ref[...]
    h = _rms(x, g_ref[0:1, :])
    hp = jnp.where(pos0 > 0, _rms(xp_ref[...], g_ref[0:1, :]), 0.0)
    hn = jnp.where(pos0 + tm < seq, _rms(xn_ref[...], g_ref[0:1, :]), 0.0)
    ext = jnp.concatenate([hp, h, hn], axis=0)
    n_ext = tm + 2 * POOL_HALO
    tpos = pos0 + lax.broadcasted_iota(jnp.int32, (tm, gw), 0)

    def ahead(a, j):
        return pltpu.roll(a, (n_ext - j) % n_ext, axis=0)

    ys = []
    for g, w in enumerate(POOL_WINDOWS):
        cs = slice(g * gw, (g + 1) * gw)
        acc = ext[:, cs]
        if POOL_HALO - w // 2:
            acc = ahead(acc, POOL_HALO - w // 2)
        span = 1
        while span < w:
            acc = acc + ahead(acc, span)
            span *= 2
        cnt = (jnp.minimum(tpos + w // 2, seq) - jnp.maximum(tpos - w // 2, 0)).astype(F32)
        pooled = acc[0:tm, :] / cnt - h[:, cs]
        ys.append(jnp.dot(pooled.astype(BF16), w_ref[g], preferred_element_type=F32))
    y = (jnp.concatenate(ys, axis=1) + bs_ref[0:1, :]) * bs_ref[1:2, :]
    o_ref[...] = x + _rms(y, g_ref[1:2, :])


def _pool(x, w_pool, gains, bias_scale, *, tm, seq):
    t, d = x.shape
    hb = tm // POOL_HALO
    last = t // POOL_HALO - 1
    n_groups, gw, _ = w_pool.shape
    return pl.pallas_call(
        functools.partial(_pool_kernel, seq=seq),
        grid=(t // tm,),
        in_specs=[
            pl.BlockSpec((POOL_HALO, d), lambda i: (jnp.maximum(i * hb - 1, 0), 0)),
            pl.BlockSpec((tm, d), lambda i: (i, 0)),
            pl.BlockSpec((POOL_HALO, d), lambda i: (jnp.minimum((i + 1) * hb, last), 0)),
            pl.BlockSpec((n_groups, gw, gw), lambda i: (0, 0, 0)),
            pl.BlockSpec((2, d), lambda i: (0, 0)),
            pl.BlockSpec((2, d), lambda i: (0, 0)),
        ],
        out_specs=pl.BlockSpec((tm, d), lambda i: (i, 0)),
        out_shape=jax.ShapeDtypeStruct((t, d), F32),
        compiler_params=_params(("parallel",)),
        name="pool_mixer",
    )(x, x, x, w_pool, gains, bias_scale)


def _tile(n, target, quantum):
    best = None
    for cand in range(quantum, min(n, target) + 1, quantum):
        if n % cand == 0:
            best = cand
    assert best is not None, (n, target, quantum)
    return best


def kernel(x, ffn_w_gate, ffn_w_up, ffn_w_down, norm_gain, gla_w_in, gla_w_gate_up, gla_b_gate,
           gla_head_gain, gla_w_out, pool_w, pool_b, pool_scale):
    batch, seq, d = x.shape
    depth = ffn_w_gate.shape[0]
    d_ff = ffn_w_gate.shape[-1]
    gla_k = gla_w_gate_up.shape[-1]
    gla_v = gla_w_out.shape[1]
    t = batch * seq
    assert seq % GLA_CHUNK == 0

    tm = _tile(seq, 512, 16)
    tm_ffn = _tile(seq, 1024, 16)
    tf = _tile(d_ff, 512, LANES)
    tn = _tile(2 * gla_k + 2 * gla_v, 2048, LANES)
    blk = _tile(seq, 256, GLA_CHUNK)

    wg = ffn_w_gate.astype(BF16)
    wu = ffn_w_up.astype(BF16)
    wd = ffn_w_down.astype(BF16)

    xs = x.reshape(t, d)
    for i in range(depth):
        g = norm_gain[i]
        xs = _ffn(xs, wg, wu, wd, g[0:2], i, 0, tm=tm_ffn, tf=tf)
        j = i // 2
        if i % 2 == 0:
            n_main = 2 * gla_k + 2 * gla_v
            w_in = gla_w_in[j]
            w_main = w_in[:, :n_main].astype(BF16)
            w_z = jnp.pad(w_in[:, n_main:], ((0, 0), (0, LANES - 2 * GLA_RANK))).astype(BF16)
            wup = gla_w_gate_up[j].astype(BF16)
            wup_f = jnp.pad(wup[0], ((0, LANES - GLA_RANK), (0, 0)))
            wup_b = jnp.pad(wup[1], ((GLA_RANK, LANES - 2 * GLA_RANK), (0, 0)))
            proj, z = _inproj(xs, w_main, w_z, g[2:3], tm=tm, tn=tn)
            o_f, o_b = _gla_scan(proj, z, wup_f, wup_b, gla_b_gate[j], batch=batch, seq=seq,
                                 gla_k=gla_k, gla_v=gla_v, blk=blk)
            xs = _gla_out(xs, o_f, o_b, proj, gla_head_gain[j][None, :],
                          gla_w_out[j].astype(BF16), g[3:4], tm=tm, gla_k=gla_k, gla_v=gla_v)
        else:
            xs = _pool(xs, pool_w[j].astype(BF16), g[2:4],
                       jnp.stack([pool_b[j], pool_scale[j]]), tm=tm, seq=seq)
        xs = _ffn(xs, wg, wu, wd, g[4:6], i, 1, tm=tm_ffn, tf=tf)
    return xs.reshape(batch, seq, d)
```

```python
import functools

import jax
import jax.numpy as jnp
from jax import lax
from jax.experimental import pallas as pl
from jax.experimental.pallas import tpu as pltpu

NORM_EPS = 1e-6
GLA_HEADS = 4
GLA_RANK = 16
GLA_TAU = 16.0
GLA_CHUNK = 64
POOL_WINDOWS = (2, 4, 8, 16)
POOL_HALO = 8
ROW_SPLIT = 4

LANES = 128
V7X_VMEM_LIMIT_BYTES = 62 * 1024 * 1024

F32 = jnp.float32
BF16 = jnp.bfloat16


def _rms(x, gain_row):
    ms = jnp.mean(x * x, axis=-1, keepdims=True)
    return x * lax.rsqrt(ms + NORM_EPS) * gain_row


def _params(semantics):
    return pltpu.CompilerParams(dimension_semantics=semantics,
                                vmem_limit_bytes=V7X_VMEM_LIMIT_BYTES)


def _ffn_kernel(x_ref, wg_ref, wu_ref, wd_ref, g_ref, o_ref, h_ref):
    f = pl.program_id(1)
    last = pl.num_programs(1) - 1
    tm = x_ref.shape[0]
    rows = tm // ROW_SPLIT

    def swiglu(rs):
        h = h_ref[rs, :]
        a = jnp.dot(h, wg_ref[...], preferred_element_type=F32)
        b = jnp.dot(h, wu_ref[...], preferred_element_type=F32)
        c = (a * jax.nn.sigmoid(a) * b).astype(BF16)
        return jnp.dot(c, wd_ref[...], preferred_element_type=F32)

    @pl.when(f == 0)
    def _():
        for q in range(ROW_SPLIT):
            rs = slice(q * rows, (q + 1) * rows)
            h_ref[rs, :] = _rms(x_ref[rs, :], g_ref[0:1, :]).astype(BF16)
            o_ref[rs, :] = swiglu(rs)

    @pl.when((f > 0) & (f < last))
    def _():
        o_ref[...] += swiglu(slice(None))

    @pl.when(f == last)
    def _():
        for q in range(ROW_SPLIT):
            rs = slice(q * rows, (q + 1) * rows)
            y = o_ref[rs, :] + swiglu(rs)
            o_ref[rs, :] = x_ref[rs, :] + 0.5 * _rms(y, g_ref[1:2, :])


def _ffn(x, wg, wu, wd, gains, layer, which, *, tm, tf):
    t, d = x.shape
    d_ff = wg.shape[-1]
    grid = (t // tm, d_ff // tf)
    assert grid[1] >= 2 and tm % (16 * ROW_SPLIT) == 0
    return pl.pallas_call(
        _ffn_kernel,
        grid=grid,
        in_specs=[
            pl.BlockSpec((tm, d), lambda i, f: (i, 0)),
            pl.BlockSpec((None, None, d, tf), lambda i, f: (layer, which, 0, f)),
            pl.BlockSpec((None, None, d, tf), lambda i, f: (layer, which, 0, f)),
            pl.BlockSpec((None, None, tf, d), lambda i, f: (layer, which, f, 0)),
            pl.BlockSpec((2, d), lambda i, f: (0, 0)),
        ],
        out_specs=pl.BlockSpec((tm, d), lambda i, f: (i, 0)),
        out_shape=jax.ShapeDtypeStruct((t, d), F32),
        scratch_shapes=[pltpu.VMEM((tm, d), BF16)],
        compiler_params=_params(("parallel", "arbitrary")),
        name="ffn",
    )(x, wg, wu, wd, gains)


def _inproj_kernel(x_ref, w_ref, wz_ref, g_ref, p_ref, z_ref, h_ref):
    n = pl.program_id(1)
    rows = x_ref.shape[0] // ROW_SPLIT

    @pl.when(n == 0)
    def _():
        for q in range(ROW_SPLIT):
            rs = slice(q * rows, (q + 1) * rows)
            h = _rms(x_ref[rs, :], g_ref[...]).astype(BF16)
            h_ref[rs, :] = h
            z_ref[rs, :] = jnp.dot(h, wz_ref[...], preferred_element_type=F32)
            p_ref[rs, :] = jnp.dot(h, w_ref[...], preferred_element_type=F32).astype(BF16)

    @pl.when(n > 0)
    def _():
        p_ref[...] = jnp.dot(h_ref[...], w_ref[...], preferred_element_type=F32).astype(BF16)


def _inproj(x, w_main, w_z, gain, *, tm, tn):
    t, d = x.shape
    n_main = w_main.shape[1]
    return pl.pallas_call(
        _inproj_kernel,
        grid=(t // tm, n_main // tn),
        in_specs=[
            pl.BlockSpec((tm, d), lambda i, n: (i, 0)),
            pl.BlockSpec((d, tn), lambda i, n: (0, n)),
            pl.BlockSpec((d, LANES), lambda i, n: (0, 0)),
            pl.BlockSpec((1, d), lambda i, n: (0, 0)),
        ],
        out_specs=[
            pl.BlockSpec((tm, tn), lambda i, n: (i, n)),
            pl.BlockSpec((tm, LANES), lambda i, n: (i, 0)),
        ],
        out_shape=[jax.ShapeDtypeStruct((t, n_main), BF16),
                   jax.ShapeDtypeStruct((t, LANES), F32)],
        scratch_shapes=[pltpu.VMEM((tm, d), BF16)],
        compiler_params=_params(("parallel", "arbitrary")),
        name="gla_inproj",
    )(x, w_main, w_z, gain)


def _gla_direction(q_ref, k_ref, v_ref, z_ref, wup_ref, bgate_row, s_ref, o_ref, *, reverse):
    c = GLA_CHUNK
    blk, hk = q_ref.shape
    nb = blk // c
    dk = hk // GLA_HEADS
    dv = v_ref.shape[1] // GLA_HEADS
    rows = lambda j: slice(j * c, (j + 1) * c)

    row = lax.broadcasted_iota(jnp.int32, (blk, blk), 0)
    col = lax.broadcasted_iota(jnp.int32, (blk, blk), 1)
    same_chunk = (row // c) == (col // c)
    tri = (col >= row) if reverse else (col <= row)
    ones_tri = jnp.where(same_chunk, jnp.where(tri, 1.0, 0.0), 0.0).astype(BF16)
    score_mask = tri

    gate = jnp.dot(z_ref[...].astype(BF16), wup_ref[...], preferred_element_type=F32) + bgate_row
    log_a = (jnp.minimum(gate, 0.0) - jnp.log(1.0 + jnp.exp(-jnp.abs(gate)))) * (1.0 / GLA_TAU)
    b = jnp.dot(ones_tri, log_a.astype(BF16), preferred_element_type=F32)
    g = [b[j * c:j * c + 1, :] if reverse else b[(j + 1) * c - 1:(j + 1) * c, :] for j in range(nb)]
    eg = [jnp.exp(gj) for gj in g]
    g_rows = jnp.concatenate([jnp.broadcast_to(gj, (c, hk)) for gj in g], axis=0)

    q = q_ref[...].astype(F32)
    k = k_ref[...].astype(F32)
    q_in = q * jnp.exp(b) * (dk ** -0.5)
    k_in = (k * jnp.exp(-b)).astype(BF16)
    k_end = k * jnp.exp(g_rows - b)
    q_in_b = q_in.astype(BF16)

    order = list(range(nb))[::-1] if reverse else list(range(nb))
    kk = {}
    keys = {}
    q_blk = [None] * nb
    run = None
    for p, i in enumerate(order):
        keys[i] = {j: kk[j].astype(BF16) for j in order[:p]}
        q_blk[i] = q_in[rows(i)] if run is None else q_in[rows(i)] * run
        for j in order[:p]:
            kk[j] = kk[j] * eg[i]
        kk[i] = k_end[rows(i)]
        run = eg[i] if run is None else run * eg[i]
    decay = run
    k_blk = jnp.concatenate([kk[j] for j in range(nb)], axis=0).astype(BF16)
    q_blk = jnp.concatenate(q_blk, axis=0).astype(BF16)
    zeros = jnp.zeros((c, dk), BF16)

    for h in range(GLA_HEADS):
        ks = slice(h * dk, (h + 1) * dk)
        vs = slice(h * dv, (h + 1) * dv)
        v_h = v_ref[:, vs]
        score_rows = []
        for i in range(nb):
            key_i = jnp.concatenate(
                [k_in[rows(i), ks] if j == i else (keys[i][j][:, ks] if j in keys[i] else zeros)
                 for j in range(nb)], axis=0)
            score_rows.append(lax.dot_general(q_in_b[rows(i), ks], key_i, (((1,), (1,)), ((), ())),
                                              preferred_element_type=F32))
        scores = jnp.where(score_mask, jnp.concatenate(score_rows, axis=0), 0.0).astype(BF16)
        state = s_ref[h]
        o = (jnp.dot(scores, v_h, preferred_element_type=F32)
             + jnp.dot(q_blk[:, ks], state.astype(BF16), preferred_element_type=F32))
        o_ref[:, vs] = o.astype(o_ref.dtype)
        upd = lax.dot_general(k_blk[:, ks], v_h, (((0,), (0,)), ((), ())),
                              preferred_element_type=F32)
        dcol = jnp.transpose(jnp.broadcast_to(decay[:, ks], (LANES, dk)))
        dmat = jnp.concatenate([dcol] * (dv // LANES), axis=1) if dv > LANES else dcol[:, :dv]
        s_ref[h] = state * dmat + upd


def _gla_scan_kernel(qf_ref, kf_ref, vf_ref, zf_ref, qb_ref, kb_ref, vb_ref, zb_ref,
                     wupf_ref, wupb_ref, bg_ref, of_ref, ob_ref, sf_ref, sb_ref):
    @pl.when(pl.program_id(1) == 0)
    def _():
        sf_ref[...] = jnp.zeros_like(sf_ref)
        sb_ref[...] = jnp.zeros_like(sb_ref)

    _gla_direction(qf_ref, kf_ref, vf_ref, zf_ref, wupf_ref, bg_ref[0:1, :], sf_ref, of_ref,
                   reverse=False)
    _gla_direction(qb_ref, kb_ref, vb_ref, zb_ref, wupb_ref, bg_ref[1:2, :], sb_ref, ob_ref,
                   reverse=True)


def _gla_scan(proj, z, wup_f, wup_b, b_gate, *, batch, seq, gla_k, gla_v, blk):
    t = proj.shape[0]
    n = seq // blk
    dk = gla_k // GLA_HEADS
    dv = gla_v // GLA_HEADS
    v_col = (2 * gla_k) // gla_v
    assert v_col * gla_v == 2 * gla_k and blk % GLA_CHUNK == 0

    def fwd(col):
        return lambda b, i: (b * n + i, col)

    def bwd(col):
        return lambda b, i: (b * n + (n - 1 - i), col)

    const = lambda b, i: (0, 0)
    in_specs = []
    for m in (fwd, bwd):
        in_specs += [
            pl.BlockSpec((blk, gla_k), m(0)),
            pl.BlockSpec((blk, gla_k), m(1)),
            pl.BlockSpec((blk, gla_v), m(v_col)),
            pl.BlockSpec((blk, LANES), m(0)),
        ]
    in_specs += [
        pl.BlockSpec((LANES, gla_k), const),
        pl.BlockSpec((LANES, gla_k), const),
        pl.BlockSpec((2, gla_k), const),
    ]
    return pl.pallas_call(
        _gla_scan_kernel,
        grid=(batch, n),
        in_specs=in_specs,
        out_specs=[pl.BlockSpec((blk, gla_v), fwd(0)), pl.BlockSpec((blk, gla_v), bwd(0))],
        out_shape=[jax.ShapeDtypeStruct((t, gla_v), BF16)] * 2,
        scratch_shapes=[pltpu.VMEM((GLA_HEADS, dk, dv), F32)] * 2,
        compiler_params=_params(("parallel", "arbitrary")),
        name="gla_scan",
    )(proj, proj, proj, z, proj, proj, proj, z, wup_f, wup_b, b_gate)


def _gla_out_kernel(x_ref, of_ref, ob_ref, r_ref, hg_ref, w_ref, g_ref, o_ref):
    dv = hg_ref.shape[1]
    parts = []
    for h in range(GLA_HEADS):
        vs = slice(h * dv, (h + 1) * dv)
        o = of_ref[:, vs].astype(F32) + ob_ref[:, vs].astype(F32)
        o = _rms(o, hg_ref[...])
        r = r_ref[:, vs].astype(F32)
        parts.append((o * (r * jax.nn.sigmoid(r))).astype(BF16))
    gated = jnp.concatenate(parts, axis=1)
    y = jnp.dot(gated, w_ref[...], preferred_element_type=F32)
    o_ref[...] = x_ref[...] + _rms(y, g_ref[...])


def _gla_out(x, o_f, o_b, proj, head_gain, w_out, gain, *, tm, gla_k, gla_v):
    t, d = x.shape
    r_col = (2 * gla_k + gla_v) // gla_v
    assert r_col * gla_v == 2 * gla_k + gla_v
    return pl.pallas_call(
        _gla_out_kernel,
        grid=(t // tm,),
        in_specs=[
            pl.BlockSpec((tm, d), lambda i: (i, 0)),
            pl.BlockSpec((tm, gla_v), lambda i: (i, 0)),
            pl.BlockSpec((tm, gla_v), lambda i: (i, 0)),
            pl.BlockSpec((tm, gla_v), lambda i: (i, r_col)),
            pl.BlockSpec((1, gla_v // GLA_HEADS), lambda i: (0, 0)),
            pl.BlockSpec((gla_v, d), lambda i: (0, 0)),
            pl.BlockSpec((1, d), lambda i: (0, 0)),
        ],
        out_specs=pl.BlockSpec((tm, d), lambda i: (i, 0)),
        out_shape=jax.ShapeDtypeStruct((t, d), F32),
        compiler_params=_params(("parallel",)),
        name="gla_out",
    )(x, o_f, o_b, proj, head_gain, w_out, gain)


def _pool_kernel(xp_ref, x_ref, xn_ref, w_ref, g_ref, bs_ref, o_ref, *, seq):
    tm, d = x_ref.shape
    n_groups = len(POOL_WINDOWS)
    gw = d // n_groups
    pos0 = (pl.program_id(0) % (seq // tm)) * tm
    x = x_ref[...]
    h = _rms(x, g_ref[0:1, :])
    hp = jnp.where(pos0 > 0, _rms(xp_ref[...], g_ref[0:1, :]), 0.0)
    hn = jnp.where(pos0 + tm < seq, _rms(xn_ref[...], g_ref[0:1, :]), 0.0)
    ext = jnp.concatenate([hp, h, hn], axis=0)
    n_ext = tm + 2 * POOL_HALO
    tpos = pos0 + lax.broadcasted_iota(jnp.int32, (tm, gw), 0)

    def ahead(a, j):
        return pltpu.roll(a, (n_ext - j) % n_ext, axis=0)

    ys = []
    for g, w in enumerate(POOL_WINDOWS):
        cs = slice(g * gw, (g + 1) * gw)
        acc = ext[:, cs]
        if POOL_HALO - w // 2:
            acc = ahead(acc, POOL_HALO - w // 2)
        span = 1
        while span < w:
            acc = acc + ahead(acc, span)
            span *= 2
        cnt = (jnp.minimum(tpos + w // 2, seq) - jnp.maximum(tpos - w // 2, 0)).astype(F32)
        pooled = acc[0:tm, :] / cnt - h[:, cs]
        ys.append(jnp.dot(pooled.astype(BF16), w_ref[g], preferred_element_type=F32))
    y = (jnp.concatenate(ys, axis=1) + bs_ref[0:1, :]) * bs_ref[1:2, :]
    o_ref[...] = x + _rms(y, g_ref[1:2, :])


def _pool(x, w_pool, gains, bias_scale, *, tm, seq):
    t, d = x.shape
    hb = tm // POOL_HALO
    last = t // POOL_HALO - 1
    n_groups, gw, _ = w_pool.shape
    return pl.pallas_call(
        functools.partial(_pool_kernel, seq=seq),
        grid=(t // tm,),
        in_specs=[
            pl.BlockSpec((POOL_HALO, d), lambda i: (jnp.maximum(i * hb - 1, 0), 0)),
            pl.BlockSpec((tm, d), lambda i: (i, 0)),
            pl.BlockSpec((POOL_HALO, d), lambda i: (jnp.minimum((i + 1) * hb, last), 0)),
            pl.BlockSpec((n_groups, gw, gw), lambda i: (0, 0, 0)),
            pl.BlockSpec((2, d), lambda i: (0, 0)),
            pl.BlockSpec((2, d), lambda i: (0, 0)),
        ],
        out_specs=pl.BlockSpec((tm, d), lambda i: (i, 0)),
        out_shape=jax.ShapeDtypeStruct((t, d), F32),
        compiler_params=_params(("parallel",)),
        name="pool_mixer",
    )(x, x, x, w_pool, gains, bias_scale)


def _tile(n, target, quantum):
    best = None
    for cand in range(quantum, min(n, target) + 1, quantum):
        if n % cand == 0:
            best = cand
    assert best is not None, (n, target, quantum)
    return best


def kernel(x, ffn_w_gate, ffn_w_up, ffn_w_down, norm_gain, gla_w_in, gla_w_gate_up, gla_b_gate,
           gla_head_gain, gla_w_out, pool_w, pool_b, pool_scale):
    batch, seq, d = x.shape
    depth = ffn_w_gate.shape[0]
    d_ff = ffn_w_gate.shape[-1]
    gla_k = gla_w_gate_up.shape[-1]
    gla_v = gla_w_out.shape[1]
    t = batch * seq
    assert seq % GLA_CHUNK == 0

    tm = _tile(seq, 512, 16)
    tm_ffn = _tile(seq, 1024, 16)
    tf = _tile(d_ff, 512, LANES)
    tn = _tile(2 * gla_k + 2 * gla_v, 2048, LANES)
    blk = _tile(seq, 256, GLA_CHUNK)

    wg = ffn_w_gate.astype(BF16)
    wu = ffn_w_up.astype(BF16)
    wd = ffn_w_down.astype(BF16)

    xs = x.reshape(t, d)
    for i in range(depth):
        g = norm_gain[i]
        xs = _ffn(xs, wg, wu, wd, g[0:2], i, 0, tm=tm_ffn, tf=tf)
        j = i // 2
        if i % 2 == 0:
            n_main = 2 * gla_k + 2 * gla_v
            w_in = gla_w_in[j]
            w_main = w_in[:, :n_main].astype(BF16)
            w_z = jnp.pad(w_in[:, n_main:], ((0, 0), (0, LANES - 2 * GLA_RANK))).astype(BF16)
            wup = gla_w_gate_up[j].astype(BF16)
            wup_f = jnp.pad(wup[0], ((0, LANES - GLA_RANK), (0, 0)))
            wup_b = jnp.pad(wup[1], ((GLA_RANK, LANES - 2 * GLA_RANK), (0, 0)))
            proj, z = _inproj(xs, w_main, w_z, g[2:3], tm=tm_ffn, tn=tn)
            o_f, o_b = _gla_scan(proj, z, wup_f, wup_b, gla_b_gate[j], batch=batch, seq=seq,
                                 gla_k=gla_k, gla_v=gla_v, blk=blk)
            xs = _gla_out(xs, o_f, o_b, proj, gla_head_gain[j][None, :],
                          gla_w_out[j].astype(BF16), g[3:4], tm=tm, gla_k=gla_k, gla_v=gla_v)
        else:
            xs = _pool(xs, pool_w[j].astype(BF16), g[2:4],
                       jnp.stack([pool_b[j], pool_scale[j]]), tm=tm, seq=seq)
        xs = _ffn(xs, wg, wu, wd, g[4:6], i, 1, tm=tm_ffn, tf=tf)
    return xs.reshape(batch, seq, d)
```

```python
import functools

import jax
import jax.numpy as jnp
from jax import lax
from jax.experimental import pallas as pl
from jax.experimental.pallas import tpu as pltpu

NORM_EPS = 1e-6
GLA_HEADS = 4
GLA_RANK = 16
GLA_TAU = 16.0
GLA_CHUNK = 64
POOL_WINDOWS = (2, 4, 8, 16)
POOL_HALO = 8
ROW_SPLIT = 4

LANES = 128
V7X_VMEM_LIMIT_BYTES = 62 * 1024 * 1024

F32 = jnp.float32
BF16 = jnp.bfloat16


def _rms(x, gain_row):
    ms = jnp.mean(x * x, axis=-1, keepdims=True)
    return x * lax.rsqrt(ms + NORM_EPS) * gain_row


def _params(semantics):
    return pltpu.CompilerParams(dimension_semantics=semantics,
                                vmem_limit_bytes=V7X_VMEM_LIMIT_BYTES)


def _ffn_kernel(x_ref, wg_ref, wu_ref, wd_ref, g_ref, o_ref, h_ref):
    f = pl.program_id(1)
    last = pl.num_programs(1) - 1
    tm = x_ref.shape[0]
    rows = tm // ROW_SPLIT

    def swiglu(rs):
        h = h_ref[rs, :]
        a = jnp.dot(h, wg_ref[...], preferred_element_type=F32)
        b = jnp.dot(h, wu_ref[...], preferred_element_type=F32)
        c = (a * jax.nn.sigmoid(a) * b).astype(BF16)
        return jnp.dot(c, wd_ref[...], preferred_element_type=F32)

    @pl.when(f == 0)
    def _():
        for q in range(ROW_SPLIT):
            rs = slice(q * rows, (q + 1) * rows)
            h_ref[rs, :] = _rms(x_ref[rs, :], g_ref[0:1, :]).astype(BF16)
            o_ref[rs, :] = swiglu(rs)

    @pl.when((f > 0) & (f < last))
    def _():
        o_ref[...] += swiglu(slice(None))

    @pl.when(f == last)
    def _():
        for q in range(ROW_SPLIT):
            rs = slice(q * rows, (q + 1) * rows)
            y = o_ref[rs, :] + swiglu(rs)
            o_ref[rs, :] = x_ref[rs, :] + 0.5 * _rms(y, g_ref[1:2, :])


def _ffn(x, wg, wu, wd, gains, layer, which, *, tm, tf):
    t, d = x.shape
    d_ff = wg.shape[-1]
    grid = (t // tm, d_ff // tf)
    assert grid[1] >= 2 and tm % (16 * ROW_SPLIT) == 0
    last = grid[1] - 1

    def chunk(i, f):
        return jnp.where(i % 2 == 0, f, last - f)

    return pl.pallas_call(
        _ffn_kernel,
        grid=grid,
        in_specs=[
            pl.BlockSpec((tm, d), lambda i, f: (i, 0)),
            pl.BlockSpec((None, None, d, tf), lambda i, f: (layer, which, 0, chunk(i, f))),
            pl.BlockSpec((None, None, d, tf), lambda i, f: (layer, which, 0, chunk(i, f))),
            pl.BlockSpec((None, None, tf, d), lambda i, f: (layer, which, chunk(i, f), 0)),
            pl.BlockSpec((2, d), lambda i, f: (0, 0)),
        ],
        out_specs=pl.BlockSpec((tm, d), lambda i, f: (i, 0)),
        out_shape=jax.ShapeDtypeStruct((t, d), F32),
        scratch_shapes=[pltpu.VMEM((tm, d), BF16)],
        compiler_params=_params(("parallel", "arbitrary")),
        name="ffn",
    )(x, wg, wu, wd, gains)


def _inproj_kernel(x_ref, w_ref, wz_ref, g_ref, p_ref, z_ref, h_ref):
    n = pl.program_id(1)
    rows = x_ref.shape[0] // ROW_SPLIT

    @pl.when(n == 0)
    def _():
        for q in range(ROW_SPLIT):
            rs = slice(q * rows, (q + 1) * rows)
            h = _rms(x_ref[rs, :], g_ref[...]).astype(BF16)
            h_ref[rs, :] = h
            z_ref[rs, :] = jnp.dot(h, wz_ref[...], preferred_element_type=F32)
            p_ref[rs, :] = jnp.dot(h, w_ref[...], preferred_element_type=F32).astype(BF16)

    @pl.when(n > 0)
    def _():
        p_ref[...] = jnp.dot(h_ref[...], w_ref[...], preferred_element_type=F32).astype(BF16)


def _inproj(x, w_in, w_z, gain, *, n_main, tm, tn):
    t, d = x.shape
    return pl.pallas_call(
        _inproj_kernel,
        grid=(t // tm, n_main // tn),
        in_specs=[
            pl.BlockSpec((tm, d), lambda i, n: (i, 0)),
            pl.BlockSpec((d, tn), lambda i, n: (0, n)),
            pl.BlockSpec((d, LANES), lambda i, n: (0, 0)),
            pl.BlockSpec((1, d), lambda i, n: (0, 0)),
        ],
        out_specs=[
            pl.BlockSpec((tm, tn), lambda i, n: (i, n)),
            pl.BlockSpec((tm, LANES), lambda i, n: (i, 0)),
        ],
        out_shape=[jax.ShapeDtypeStruct((t, n_main), BF16),
                   jax.ShapeDtypeStruct((t, LANES), F32)],
        scratch_shapes=[pltpu.VMEM((tm, d), BF16)],
        compiler_params=_params(("parallel", "arbitrary")),
        name="gla_inproj",
    )(x, w_in, w_z, gain)


def _gla_direction(q_ref, k_ref, v_ref, z_ref, wup_ref, bgate_row, s_ref, o_ref, *, reverse):
    c = GLA_CHUNK
    blk, hk = q_ref.shape
    nb = blk // c
    dk = hk // GLA_HEADS
    dv = v_ref.shape[1] // GLA_HEADS
    rows = lambda j: slice(j * c, (j + 1) * c)

    row = lax.broadcasted_iota(jnp.int32, (blk, blk), 0)
    col = lax.broadcasted_iota(jnp.int32, (blk, blk), 1)
    same_chunk = (row // c) == (col // c)
    tri = (col >= row) if reverse else (col <= row)
    tri_tau = jnp.where(same_chunk, jnp.where(tri, 1.0 / GLA_TAU, 0.0), 0.0).astype(BF16)
    score_mask = tri

    gate = jnp.dot(z_ref[...].astype(BF16), wup_ref[...], preferred_element_type=F32) + bgate_row
    log_sig = jnp.minimum(gate, 0.0) - jnp.log(1.0 + jnp.exp(-jnp.abs(gate)))
    b = jnp.dot(tri_tau, log_sig.astype(BF16), preferred_element_type=F32)
    g = [b[j * c:j * c + 1, :] if reverse else b[(j + 1) * c - 1:(j + 1) * c, :] for j in range(nb)]
    eg = [jnp.exp(gj) for gj in g]
    g_rows = jnp.concatenate([jnp.broadcast_to(gj, (c, hk)) for gj in g], axis=0)

    q = q_ref[...].astype(F32)
    k = k_ref[...].astype(F32)
    q_in = q * jnp.exp(b) * (dk ** -0.5)
    k_in = (k * jnp.exp(-b)).astype(BF16)
    k_end = k * jnp.exp(g_rows - b)
    q_in_b = q_in.astype(BF16)

    order = list(range(nb))[::-1] if reverse else list(range(nb))
    kk = {}
    keys = {}
    q_blk = [None] * nb
    run = None
    for p, i in enumerate(order):
        keys[i] = {j: kk[j].astype(BF16) for j in order[:p]}
        q_blk[i] = q_in[rows(i)] if run is None else q_in[rows(i)] * run
        for j in order[:p]:
            kk[j] = kk[j] * eg[i]
        kk[i] = k_end[rows(i)]
        run = eg[i] if run is None else run * eg[i]
    decay = run
    k_blk = jnp.concatenate([kk[j] for j in range(nb)], axis=0).astype(BF16)
    q_blk = jnp.concatenate(q_blk, axis=0).astype(BF16)
    zeros = jnp.zeros((c, dk), BF16)

    for h in range(GLA_HEADS):
        ks = slice(h * dk, (h + 1) * dk)
        vs = slice(h * dv, (h + 1) * dv)
        v_h = v_ref[:, vs]
        score_rows = []
        for i in range(nb):
            key_i = jnp.concatenate(
                [k_in[rows(i), ks] if j == i else (keys[i][j][:, ks] if j in keys[i] else zeros)
                 for j in range(nb)], axis=0)
            score_rows.append(lax.dot_general(q_in_b[rows(i), ks], key_i, (((1,), (1,)), ((), ())),
                                              preferred_element_type=F32))
        scores = jnp.where(score_mask, jnp.concatenate(score_rows, axis=0), 0.0).astype(BF16)
        state = s_ref[h]
        o = (jnp.dot(scores, v_h, preferred_element_type=F32)
             + jnp.dot(q_blk[:, ks], state.astype(BF16), preferred_element_type=F32))
        o_ref[:, vs] = o.astype(o_ref.dtype)
        upd = lax.dot_general(k_blk[:, ks], v_h, (((0,), (0,)), ((), ())),
                              preferred_element_type=F32)
        dcol = jnp.transpose(jnp.broadcast_to(decay[:, ks], (LANES, dk)))
        dmat = jnp.concatenate([dcol] * (dv // LANES), axis=1) if dv > LANES else dcol[:, :dv]
        s_ref[h] = state * dmat + upd


def _gla_scan_kernel(qf_ref, kf_ref, vf_ref, zf_ref, qb_ref, kb_ref, vb_ref, zb_ref,
                     wupf_ref, wupb_ref, bg_ref, of_ref, ob_ref, sf_ref, sb_ref):
    @pl.when(pl.program_id(1) == 0)
    def _():
        sf_ref[...] = jnp.zeros_like(sf_ref)
        sb_ref[...] = jnp.zeros_like(sb_ref)

    _gla_direction(qf_ref, kf_ref, vf_ref, zf_ref, wupf_ref, bg_ref[0:1, :], sf_ref, of_ref,
                   reverse=False)
    _gla_direction(qb_ref, kb_ref, vb_ref, zb_ref, wupb_ref, bg_ref[1:2, :], sb_ref, ob_ref,
                   reverse=True)


def _gla_scan(proj, z, wup_f, wup_b, b_gate, *, batch, seq, gla_k, gla_v, blk):
    t = proj.shape[0]
    n = seq // blk
    dk = gla_k // GLA_HEADS
    dv = gla_v // GLA_HEADS
    v_col = (2 * gla_k) // gla_v
    assert v_col * gla_v == 2 * gla_k and blk % GLA_CHUNK == 0

    def fwd(col):
        return lambda b, i: (b * n + i, col)

    def bwd(col):
        return lambda b, i: (b * n + (n - 1 - i), col)

    const = lambda b, i: (0, 0)
    in_specs = []
    for m in (fwd, bwd):
        in_specs += [
            pl.BlockSpec((blk, gla_k), m(0)),
            pl.BlockSpec((blk, gla_k), m(1)),
            pl.BlockSpec((blk, gla_v), m(v_col)),
            pl.BlockSpec((blk, LANES), m(0)),
        ]
    in_specs += [
        pl.BlockSpec((LANES, gla_k), const),
        pl.BlockSpec((LANES, gla_k), const),
        pl.BlockSpec((2, gla_k), const),
    ]
    return pl.pallas_call(
        _gla_scan_kernel,
        grid=(batch, n),
        in_specs=in_specs,
        out_specs=[pl.BlockSpec((blk, gla_v), fwd(0)), pl.BlockSpec((blk, gla_v), bwd(0))],
        out_shape=[jax.ShapeDtypeStruct((t, gla_v), BF16)] * 2,
        scratch_shapes=[pltpu.VMEM((GLA_HEADS, dk, dv), F32)] * 2,
        compiler_params=_params(("parallel", "arbitrary")),
        name="gla_scan",
    )(proj, proj, proj, z, proj, proj, proj, z, wup_f, wup_b, b_gate)


def _gla_out_kernel(x_ref, of_ref, ob_ref, r_ref, hg_ref, w_ref, g_ref, o_ref):
    dv = hg_ref.shape[1]
    parts = []
    for h in range(GLA_HEADS):
        vs = slice(h * dv, (h + 1) * dv)
        o = of_ref[:, vs].astype(F32) + ob_ref[:, vs].astype(F32)
        o = _rms(o, hg_ref[...])
        r = r_ref[:, vs].astype(F32)
        parts.append((o * (r * jax.nn.sigmoid(r))).astype(BF16))
    gated = jnp.concatenate(parts, axis=1)
    y = jnp.dot(gated, w_ref[...], preferred_element_type=F32)
    o_ref[...] = x_ref[...] + _rms(y, g_ref[...])


def _gla_out(x, o_f, o_b, proj, head_gain, w_out, gain, *, tm, gla_k, gla_v):
    t, d = x.shape
    r_col = (2 * gla_k + gla_v) // gla_v
    assert r_col * gla_v == 2 * gla_k + gla_v
    return pl.pallas_call(
        _gla_out_kernel,
        grid=(t // tm,),
        in_specs=[
            pl.BlockSpec((tm, d), lambda i: (i, 0)),
            pl.BlockSpec((tm, gla_v), lambda i: (i, 0)),
            pl.BlockSpec((tm, gla_v), lambda i: (i, 0)),
            pl.BlockSpec((tm, gla_v), lambda i: (i, r_col)),
            pl.BlockSpec((1, gla_v // GLA_HEADS), lambda i: (0, 0)),
            pl.BlockSpec((gla_v, d), lambda i: (0, 0)),
            pl.BlockSpec((1, d), lambda i: (0, 0)),
        ],
        out_specs=pl.BlockSpec((tm, d), lambda i: (i, 0)),
        out_shape=jax.ShapeDtypeStruct((t, d), F32),
        compiler_params=_params(("parallel",)),
        name="gla_out",
    )(x, o_f, o_b, proj, head_gain, w_out, gain)


def _pool_kernel(xp_ref, x_ref, xn_ref, w_ref, g_ref, bs_ref, o_ref, *, seq):
    tm, d = x_ref.shape
    n_groups = len(POOL_WINDOWS)
    gw = d // n_groups
    pos0 = (pl.program_id(0) % (seq // tm)) * tm
    x = x_ref[...]
    h = _rms(x, g_ref[0:1, :])
    hp = jnp.where(pos0 > 0, _rms(xp_ref[...], g_ref[0:1, :]), 0.0)
    hn = jnp.where(pos0 + tm < seq, _rms(xn_ref[...], g_ref[0:1, :]), 0.0)
    ext = jnp.concatenate([hp, h, hn], axis=0)
    n_ext = tm + 2 * POOL_HALO
    tpos = pos0 + lax.broadcasted_iota(jnp.int32, (tm, gw), 0)

    def ahead(a, j):
        return pltpu.roll(a, (n_ext - j) % n_ext, axis=0)

    ys = []
    for g, w in enumerate(POOL_WINDOWS):
        cs = slice(g * gw, (g + 1) * gw)
        acc = ext[:, cs]
        if POOL_HALO - w // 2:
            acc = ahead(acc, POOL_HALO - w // 2)
        span = 1
        while span < w:
            acc = acc + ahead(acc, span)
            span *= 2
        cnt = (jnp.minimum(tpos + w // 2, seq) - jnp.maximum(tpos - w // 2, 0)).astype(F32)
        pooled = acc[0:tm, :] / cnt - h[:, cs]
        ys.append(jnp.dot(pooled.astype(BF16), w_ref[g], preferred_element_type=F32))
    y = (jnp.concatenate(ys, axis=1) + bs_ref[0:1, :]) * bs_ref[1:2, :]
    o_ref[...] = x + _rms(y, g_ref[1:2, :])


def _pool(x, w_pool, gains, bias_scale, *, tm, seq):
    t, d = x.shape
    hb = tm // POOL_HALO
    last = t // POOL_HALO - 1
    n_groups, gw, _ = w_pool.shape
    return pl.pallas_call(
        functools.partial(_pool_kernel, seq=seq),
        grid=(t // tm,),
        in_specs=[
            pl.BlockSpec((POOL_HALO, d), lambda i: (jnp.maximum(i * hb - 1, 0), 0)),
            pl.BlockSpec((tm, d), lambda i: (i, 0)),
            pl.BlockSpec((POOL_HALO, d), lambda i: (jnp.minimum((i + 1) * hb, last), 0)),
            pl.BlockSpec((n_groups, gw, gw), lambda i: (0, 0, 0)),
            pl.BlockSpec((2, d), lambda i: (0, 0)),
            pl.BlockSpec((2, d), lambda i: (0, 0)),
        ],
        out_specs=pl.BlockSpec((tm, d), lambda i: (i, 0)),
        out_shape=jax.ShapeDtypeStruct((t, d), F32),
        compiler_params=_params(("parallel",)),
        name="pool_mixer",
    )(x, x, x, w_pool, gains, bias_scale)


def _tile(n, target, quantum):
    best = None
    for cand in range(quantum, min(n, target) + 1, quantum):
        if n % cand == 0:
            best = cand
    assert best is not None, (n, target, quantum)
    return best


def kernel(x, ffn_w_gate, ffn_w_up, ffn_w_down, norm_gain, gla_w_in, gla_w_gate_up, gla_b_gate,
           gla_head_gain, gla_w_out, pool_w, pool_b, pool_scale):
    batch, seq, d = x.shape
    depth = ffn_w_gate.shape[0]
    d_ff = ffn_w_gate.shape[-1]
    gla_k = gla_w_gate_up.shape[-1]
    gla_v = gla_w_out.shape[1]
    t = batch * seq
    assert seq % GLA_CHUNK == 0

    tm = _tile(seq, 512, 16)
    tm_ffn = _tile(seq, 1024, 16)
    tf = _tile(d_ff, 512, LANES)
    tn = _tile(2 * gla_k + 2 * gla_v, 2048, LANES)
    blk = _tile(seq, 256, GLA_CHUNK)

    wg = ffn_w_gate.astype(BF16)
    wu = ffn_w_up.astype(BF16)
    wd = ffn_w_down.astype(BF16)

    xs = x.reshape(t, d)
    for i in range(depth):
        g = norm_gain[i]
        xs = _ffn(xs, wg, wu, wd, g[0:2], i, 0, tm=tm_ffn, tf=tf)
        j = i // 2
        if i % 2 == 0:
            n_main = 2 * gla_k + 2 * gla_v
            w_in = gla_w_in[j].astype(BF16)
            w_z = jnp.pad(w_in[:, n_main:], ((0, 0), (0, LANES - 2 * GLA_RANK)))
            wup = gla_w_gate_up[j].astype(BF16)
            wup_f = jnp.pad(wup[0], ((0, LANES - GLA_RANK), (0, 0)))
            wup_b = jnp.pad(wup[1], ((GLA_RANK, LANES - 2 * GLA_RANK), (0, 0)))
            proj, z = _inproj(xs, w_in, w_z, g[2:3], n_main=n_main, tm=tm_ffn, tn=tn)
            o_f, o_b = _gla_scan(proj, z, wup_f, wup_b, gla_b_gate[j], batch=batch, seq=seq,
                                 gla_k=gla_k, gla_v=gla_v, blk=blk)
            xs = _gla_out(xs, o_f, o_b, proj, gla_head_gain[j][None, :],
                          gla_w_out[j].astype(BF16), g[3:4], tm=tm, gla_k=gla_k, gla_v=gla_v)
        else:
            xs = _pool(xs, pool_w[j].astype(BF16), g[2:4],
                       jnp.stack([pool_b[j], pool_scale[j]]), tm=tm, seq=seq)
        xs = _ffn(xs, wg, wu, wd, g[4:6], i, 1, tm=tm_ffn, tf=tf)
    return xs.reshape(batch, seq, d)
```
